```python
import jax, jax.numpy as jnp
from jax import lax
import numpy as np

D_MODEL = 2048
BATCH = 4
SEQ = 2048
DEPTH = 4
DEC_BATCH = 128
DEC_SEQ = 4
PAST_LEN = 16384
PAGE_SIZE = 128

MIX_WIDTH = D_MODEL
LRU_WIDTH = MIX_WIDTH // 2
RWKV_WIDTH = MIX_WIDTH - LRU_WIDTH
HEAD_DIM = 64
LRU_BLOCKS = LRU_WIDTH // HEAD_DIM
RWKV_HEADS = RWKV_WIDTH // HEAD_DIM
CONV_W = 4
LRU_C = 8.0
DECAY_RANK = max(32, int(round(1.8 * RWKV_WIDTH ** 0.5 / 32)) * 32)
AAA_RANK = max(32, int(round(1.8 * RWKV_WIDTH ** 0.5 / 32)) * 32)
GATE_RANK = max(32, int(round(0.6 * RWKV_WIDTH ** 0.8 / 32)) * 32)
RWKV_COLS = 3 * RWKV_WIDTH + DECAY_RANK + AAA_RANK + GATE_RANK
IN_COLS = 2 * LRU_WIDTH + RWKV_COLS
D_FF = ((8 * D_MODEL // 3 + 127) // 128) * 128
N_EXPERTS = 8
TOP_K = 2
N_DENSE = (DEPTH + 1) // 2
N_MOE = DEPTH // 2
ALPHA = (2.0 * DEPTH) ** 0.25
BETA = (8.0 * DEPTH) ** -0.25
LN_EPS = 1e-5
GN_EPS = 64e-5

kernel_name = 'hymba_style_rglru_rwkv7_deepnorm_step'


def layer_norm(x, g, b):
    xf = x.astype(jnp.float32)
    mu = jnp.mean(xf, -1, keepdims=True)
    var = jnp.mean(jnp.square(xf - mu), -1, keepdims=True)
    return ((xf - mu) * lax.rsqrt(var + LN_EPS) * g.astype(jnp.float32) + b.astype(jnp.float32)).astype(x.dtype)


def causal_conv(xb, buf, w, b):
    T = xb.shape[1]
    xcat = jnp.concatenate([buf.astype(xb.dtype), xb], axis=1)
    out = b + sum(w[j] * xcat[:, j:j + T] for j in range(CONV_W))
    return out, xcat[:, T:]


def rg_lru(x, h0, wa, ba, wx, bx, lam):
    B, T, _ = x.shape
    xh = x.reshape(B, T, LRU_BLOCKS, HEAD_DIM)
    r = jax.nn.sigmoid((jnp.einsum('btnd,nde->btne', xh, wa).reshape(B, T, LRU_WIDTH) + ba).astype(jnp.float32))
    i = jax.nn.sigmoid((jnp.einsum('btnd,nde->btne', xh, wx).reshape(B, T, LRU_WIDTH) + bx).astype(jnp.float32))
    log_a = -LRU_C * r * jax.nn.softplus(-lam.astype(jnp.float32))
    a = jnp.exp(log_a)
    u = jnp.sqrt(-jnp.expm1(2.0 * log_a)) * (i * x.astype(jnp.float32))
    u = u.at[:, 0].add(a[:, 0] * h0.astype(jnp.float32))
    def combine(lhs, rhs):
        return (lhs[0] * rhs[0], rhs[0] * lhs[1] + rhs[1])
    _, h = lax.associative_scan(combine, (a, u), axis=1)
    return h, h[:, -1]


def wkv7_scan(S0, r, w, k, v, a, b):
    def step(S, inp):
        rt, wt, kt, vt, at, bt = inp
        sa = jnp.einsum('bhij,bhj->bhi', S, at)
        S = S * wt[:, :, None, :] + sa[..., None] * bt[:, :, None, :] + vt[..., None] * kt[:, :, None, :]
        return S, jnp.einsum('bhij,bhj->bhi', S, rt)
    xs = tuple(jnp.moveaxis(t, 1, 0) for t in (r, w, k, v, a, b))
    S_last, y = lax.scan(step, S0.astype(jnp.float32), xs)
    return jnp.moveaxis(y, 0, 1), S_last


def time_mix(x, conv_buf, h0, shift0, S0, P, l):
    B, T, _ = x.shape
    H, N, W = RWKV_HEADS, HEAD_DIM, RWKV_WIDTH
    proj = x @ P['w_in'][l]
    xb, gate_b, rw = jnp.split(proj, [LRU_WIDTH, 2 * LRU_WIDTH], axis=-1)
    xc, conv_new = causal_conv(xb, conv_buf, P['conv_w'][l], P['conv_b'][l])
    h, h_last = rg_lru(xc, h0, P['lru_wa'][l], P['lru_ba'][l], P['lru_wx'][l], P['lru_bx'][l], P['lru_lambda'][l])
    lru_out = h.astype(x.dtype) * jax.nn.gelu(gate_b)
    prev = jnp.concatenate([shift0[:, None].astype(rw.dtype), rw[:, :-1]], axis=1)
    xs = rw + (prev - rw) * P['mu_shift'][l]
    r, k, v, xw, xa, xg = jnp.split(xs, [W, 2 * W, 3 * W, 3 * W + DECAY_RANK, 3 * W + DECAY_RANK + AAA_RANK], axis=-1)
    w_log = -jax.nn.softplus(-(P['w0'][l] + jnp.tanh(xw) @ P['w2'][l]).astype(jnp.float32)) - 0.5
    decay = jnp.exp(-jnp.exp(w_log))
    a = jax.nn.sigmoid((P['a0'][l] + xa @ P['a2'][l]).astype(jnp.float32))
    g = jax.nn.sigmoid(xg) @ P['g2'][l]
    rh = r.astype(jnp.float32).reshape(B, T, H, N)
    kh = k.astype(jnp.float32).reshape(B, T, H, N)
    vh = v.astype(jnp.float32).reshape(B, T, H, N)
    ah = a.reshape(B, T, H, N)
    kk = kh * P['k_k'][l].astype(jnp.float32).reshape(H, N)
    kk = kk / jnp.maximum(jnp.sqrt(jnp.sum(kk * kk, -1, keepdims=True)), 1e-12)
    k_mod = kh * (1.0 + (ah - 1.0) * P['k_a'][l].astype(jnp.float32).reshape(H, N))
    y, S_last = wkv7_scan(S0, rh, decay.reshape(B, T, H, N), k_mod, vh, -kk, kk * ah)
    mu = jnp.mean(y, -1, keepdims=True)
    var = jnp.mean(jnp.square(y - mu), -1, keepdims=True)
    yn = (y - mu) * lax.rsqrt(var + GN_EPS) * P['gn_g'][l].astype(jnp.float32).reshape(H, N) + P['gn_b'][l].astype(jnp.float32).reshape(H, N)
    bonus = jnp.sum(rh * k_mod * P['r_k'][l].astype(jnp.float32), -1, keepdims=True) * vh
    rwkv_out = (yn + bonus).reshape(B, T, W).astype(x.dtype) * g
    out = jnp.concatenate([lru_out, rwkv_out], axis=-1) @ P['w_o'][l]
    return out, conv_new, h_last, rw[:, -1], S_last


def swiglu(x, wg, wu, wd):
    return (jax.nn.silu(x @ wg) * (x @ wu)) @ wd


def moe(x, w_router, wg, wu, wd):
    logits = (x @ w_router).astype(jnp.float32)
    top_v, top_i = lax.top_k(logits, TOP_K)
    probs = jax.nn.softmax(top_v, axis=-1)
    gates = jnp.sum(jax.nn.one_hot(top_i, N_EXPERTS, dtype=jnp.float32) * probs[..., None], axis=-2).astype(x.dtype)
    y = jnp.zeros_like(x)
    for e in range(N_EXPERTS):
        y = y + gates[..., e:e + 1] * swiglu(x, wg[e], wu[e], wd[e])
    return y


def run_trunk(x, st_conv, st_h, st_shift, st_wkv, P):
    convs, hs, shifts, wkvs = [], [], [], []
    for l in range(DEPTH):
        m, c, h, s, S = time_mix(x, st_conv[l], st_h[l], st_shift[l], st_wkv[l], P, l)
        convs.append(c.astype(x.dtype)); hs.append(h.astype(x.dtype)); shifts.append(s.astype(x.dtype)); wkvs.append(S.astype(x.dtype))
        x = layer_norm(ALPHA * x + m, P['ln1_g'][l], P['ln1_b'][l])
        if l % 2 == 0:
            f = swiglu(x, P['ffn_wg'][l // 2], P['ffn_wu'][l // 2], P['ffn_wd'][l // 2])
        else:
            f = moe(x, P['w_router'][l // 2], P['exp_wg'][l // 2], P['exp_wu'][l // 2], P['exp_wd'][l // 2])
        x = layer_norm(ALPHA * x + f, P['ln2_g'][l], P['ln2_b'][l])
    return x, jnp.stack(convs), jnp.stack(hs), jnp.stack(shifts), jnp.stack(wkvs)


def setup_inputs(seed: int = 0) -> dict:
    key = jax.random.key(seed)
    ks = iter(jax.random.split(key, 64))
    f32 = jnp.float32
    def nrm(shape, scale):
        return jax.random.normal(next(ks), shape, f32) * scale
    def uni(shape, lo, hi):
        return jax.random.uniform(next(ks), shape, f32, lo, hi)
    s = uni((DEPTH, LRU_WIDTH), 0.9, 0.999) ** (1.0 / LRU_C)
    return {
        'x_prompt': nrm((BATCH, SEQ, D_MODEL), 1.0),
        'x_sample': nrm((DEC_BATCH, DEC_SEQ, D_MODEL), 1.0),
        'state_lru_conv': nrm((DEPTH, DEC_BATCH, CONV_W - 1, LRU_WIDTH), 1.0),
        'state_lru_h': nrm((DEPTH, DEC_BATCH, LRU_WIDTH), 0.5),
        'state_rwkv_shift': nrm((DEPTH, DEC_BATCH, RWKV_COLS), 1.0),
        'state_rwkv_wkv': nrm((DEPTH, DEC_BATCH, RWKV_HEADS, HEAD_DIM, HEAD_DIM), 0.3),
        'w_in': nrm((DEPTH, D_MODEL, IN_COLS), D_MODEL ** -0.5),
        'conv_w': nrm((DEPTH, CONV_W, LRU_WIDTH), CONV_W ** -0.5),
        'conv_b': nrm((DEPTH, LRU_WIDTH), 0.01),
        'lru_wa': nrm((DEPTH, LRU_BLOCKS, HEAD_DIM, HEAD_DIM), HEAD_DIM ** -0.5),
        'lru_ba': nrm((DEPTH, LRU_WIDTH), 0.01),
        'lru_wx': nrm((DEPTH, LRU_BLOCKS, HEAD_DIM, HEAD_DIM), HEAD_DIM ** -0.5),
        'lru_bx': nrm((DEPTH, LRU_WIDTH), 0.01),
        'lru_lambda': jnp.log(s) - jnp.log1p(-s),
        'mu_shift': uni((DEPTH, RWKV_COLS), 0.0, 1.0),
        'w0': uni((DEPTH, RWKV_WIDTH), -6.0, -1.0),
        'w2': nrm((DEPTH, DECAY_RANK, RWKV_WIDTH), 0.5 * DECAY_RANK ** -0.5),
        'a0': nrm((DEPTH, RWKV_WIDTH), 0.1),
        'a2': nrm((DEPTH, AAA_RANK, RWKV_WIDTH), AAA_RANK ** -0.5),
        'g2': nrm((DEPTH, GATE_RANK, RWKV_WIDTH), GATE_RANK ** -0.5),
        'k_k': 0.85 + nrm((DEPTH, RWKV_WIDTH), 0.02),
        'k_a': 1.0 + nrm((DEPTH, RWKV_WIDTH), 0.02),
        'r_k': nrm((DEPTH, RWKV_HEADS, HEAD_DIM), 0.1),
        'gn_g': 1.0 + nrm((DEPTH, RWKV_WIDTH), 0.02),
        'gn_b': nrm((DEPTH, RWKV_WIDTH), 0.02),
        'w_o': nrm((DEPTH, MIX_WIDTH, D_MODEL), MIX_WIDTH ** -0.5 * BETA),
        'ln1_g': 1.0 + nrm((DEPTH, D_MODEL), 0.02),
        'ln1_b': nrm((DEPTH, D_MODEL), 0.02),
        'ln2_g': 1.0 + nrm((DEPTH, D_MODEL), 0.02),
        'ln2_b': nrm((DEPTH, D_MODEL), 0.02),
        'ffn_wg': nrm((N_DENSE, D_MODEL, D_FF), D_MODEL ** -0.5),
        'ffn_wu': nrm((N_DENSE, D_MODEL, D_FF), D_MODEL ** -0.5),
        'ffn_wd': nrm((N_DENSE, D_FF, D_MODEL), D_FF ** -0.5 * BETA),
        'w_router': nrm((N_MOE, D_MODEL, N_EXPERTS), D_MODEL ** -0.5),
        'exp_wg': nrm((N_MOE, N_EXPERTS, D_MODEL, D_FF), D_MODEL ** -0.5),
        'exp_wu': nrm((N_MOE, N_EXPERTS, D_MODEL, D_FF), D_MODEL ** -0.5),
        'exp_wd': nrm((N_MOE, N_EXPERTS, D_FF, D_MODEL), D_FF ** -0.5 * BETA),
    }


def reference(x_prompt, x_sample, state_lru_conv, state_lru_h, state_rwkv_shift, state_rwkv_wkv,
              w_in, conv_w, conv_b, lru_wa, lru_ba, lru_wx, lru_bx, lru_lambda,
              mu_shift, w0, w2, a0, a2, g2, k_k, k_a, r_k, gn_g, gn_b, w_o,
              ln1_g, ln1_b, ln2_g, ln2_b, ffn_wg, ffn_wu, ffn_wd,
              w_router, exp_wg, exp_wu, exp_wd):
    P = {'w_in': w_in, 'conv_w': conv_w, 'conv_b': conv_b, 'lru_wa': lru_wa, 'lru_ba': lru_ba,
         'lru_wx': lru_wx, 'lru_bx': lru_bx, 'lru_lambda': lru_lambda, 'mu_shift': mu_shift,
         'w0': w0, 'w2': w2, 'a0': a0, 'a2': a2, 'g2': g2, 'k_k': k_k, 'k_a': k_a, 'r_k': r_k,
         'gn_g': gn_g, 'gn_b': gn_b, 'w_o': w_o, 'ln1_g': ln1_g, 'ln1_b': ln1_b,
         'ln2_g': ln2_g, 'ln2_b': ln2_b, 'ffn_wg': ffn_wg, 'ffn_wu': ffn_wu, 'ffn_wd': ffn_wd,
         'w_router': w_router, 'exp_wg': exp_wg, 'exp_wu': exp_wu, 'exp_wd': exp_wd}
    Bp = x_prompt.shape[0]
    dt = x_prompt.dtype
    z_conv = jnp.zeros((DEPTH, Bp, CONV_W - 1, LRU_WIDTH), dt)
    z_h = jnp.zeros((DEPTH, Bp, LRU_WIDTH), dt)
    z_shift = jnp.zeros((DEPTH, Bp, RWKV_COLS), dt)
    z_wkv = jnp.zeros((DEPTH, Bp, RWKV_HEADS, HEAD_DIM, HEAD_DIM), dt)
    y_prompt, p_conv, p_h, p_shift, p_wkv = run_trunk(x_prompt, z_conv, z_h, z_shift, z_wkv, P)
    y_sample, s_conv, s_h, s_shift, s_wkv = run_trunk(x_sample, state_lru_conv, state_lru_h, state_rwkv_shift, state_rwkv_wkv, P)
    return (y_prompt, y_sample, p_conv, p_h, p_shift, p_wkv, s_conv, s_h, s_shift, s_wkv)
```

```python
import functools

import jax
import jax.numpy as jnp
from jax import lax
from jax.experimental import pallas as pl
from jax.experimental.pallas import tpu as pltpu

F32 = jnp.float32
BF16 = jnp.bfloat16
HI = lax.Precision.HIGHEST

HEAD_DIM = 64
CONV_W = 4
LRU_C = 8.0
N_EXPERTS = 8
LN_EPS = 1e-5
GN_EPS = 64e-5

LANES = 128
SUBLANES = 8
VMEM_LIMIT = 48 * 1024 * 1024
SAMPLE_TP = 8
TIME_ROWS = 256
TOKEN_TILE = 512
FF_TILE = 512


def _cparams(*sem):
    return pltpu.CompilerParams(dimension_semantics=sem, vmem_limit_bytes=VMEM_LIMIT)


def _sigmoid(x):
    return 1.0 / (1.0 + jnp.exp(-x))


def _softplus(x):
    return jnp.maximum(x, 0.0) + jnp.log1p(jnp.exp(-jnp.abs(x)))


def _gelu_tanh(x):
    c = 0.7978845608028654
    return 0.5 * x * (1.0 + jnp.tanh(c * (x + 0.044715 * (x * x * x))))


def _layer_norm(x, g, b):
    mu = jnp.mean(x, axis=-1, keepdims=True)
    xc = x - mu
    var = jnp.mean(xc * xc, axis=-1, keepdims=True)
    return xc * lax.rsqrt(var + LN_EPS) * g + b


def _shift_rows(x, carry, k):
    xr = pltpu.roll(x, k, 0)
    cr = pltpu.roll(carry, k, 0)
    row = lax.broadcasted_iota(jnp.int32, carry.shape, 0)
    first = jnp.where(row < k, cr, xr[:SUBLANES])
    return jnp.concatenate([first, xr[SUBLANES:]], axis=0)


def _mm_kernel(x_ref, w_ref, o_ref):
    o_ref[...] = jnp.dot(x_ref[...].astype(BF16), w_ref[...], preferred_element_type=F32)


def _matmul(x, w, tm, tn):
    M, K = x.shape
    N = w.shape[1]
    return pl.pallas_call(
        _mm_kernel,
        grid=(M // tm, N // tn),
        in_specs=[pl.BlockSpec((tm, K), lambda i, j: (i, 0)),
                  pl.BlockSpec((K, tn), lambda i, j: (0, j))],
        out_specs=pl.BlockSpec((tm, tn), lambda i, j: (i, j)),
        out_shape=jax.ShapeDtypeStruct((M, N), F32),
        compiler_params=_cparams("parallel", "parallel"),
        name="in_proj",
    )(x, w)


def _lru_kernel(*refs, blocks_per_seq, tp, lead):
    if lead:
        (xb_ref, gate_ref, h0_ref, cw_ref, cb_ref, wbd_ref, bias_ref, lam_ref,
         out_ref, h_ref, xcarry, hcarry, a_s, u_s) = refs
    else:
        (xb_ref, gate_ref, cw_ref, cb_ref, wbd_ref, bias_ref, lam_ref,
         out_ref, h_ref, xcarry, hcarry, a_s, u_s) = refs
    R, W = xb_ref.shape
    i = pl.program_id(0)

    @pl.when(i % blocks_per_seq == 0)
    def _():
        xcarry[...] = jnp.zeros_like(xcarry)
        hcarry[...] = jnp.zeros_like(hcarry)

    xb = xb_ref[...]
    carry = xcarry[...]
    cw = cw_ref[...]
    xc = cb_ref[...] + cw[3:4] * xb
    for k in range(1, CONV_W):
        xc = xc + cw[3 - k:4 - k] * _shift_rows(xb, carry, k)
    xcarry[...] = xb[R - SUBLANES:]

    xcb = xc.astype(BF16)
    nb = wbd_ref.shape[0]
    bw = W // nb
    pre_a, pre_x = [], []
    for g in range(nb):
        pre = jnp.dot(xcb[:, g * bw:(g + 1) * bw], wbd_ref[g], preferred_element_type=F32)
        pre_a.append(pre[:, :bw])
        pre_x.append(pre[:, bw:])
    bias = bias_ref[...]
    r = _sigmoid(jnp.concatenate(pre_a, axis=1) + bias[0:1])
    ig = _sigmoid(jnp.concatenate(pre_x, axis=1) + bias[1:2])
    log_a = -LRU_C * r * _softplus(-lam_ref[...])
    a = jnp.exp(log_a)
    u = jnp.sqrt(-jnp.tanh(log_a) * (a * a + 1.0)) * (ig * xc)
    if lead:
        row = lax.broadcasted_iota(jnp.int32, (R, W), 0)
        is_state = (row % tp) == (lead - 1)
        a = jnp.where(is_state, 0.0, a)
        u = jnp.where(is_state, h0_ref[...], u)
    a_s[...] = a
    u_s[...] = u

    def tile_body(t, h):
        off = pl.multiple_of(t * SUBLANES, SUBLANES)
        a8 = a_s[pl.ds(off, SUBLANES), :]
        u8 = u_s[pl.ds(off, SUBLANES), :]
        rows = []
        for j in range(SUBLANES):
            h = a8[j:j + 1] * h + u8[j:j + 1]
            rows.append(h)
        h_ref[pl.ds(off, SUBLANES), :] = jnp.concatenate(rows, axis=0)
        return h

    h_last = lax.fori_loop(0, R // SUBLANES, tile_body, hcarry[0:1])
    hcarry[...] = jnp.broadcast_to(h_last, hcarry.shape)
    out_ref[...] = h_ref[...] * _gelu_tanh(gate_ref[...])


def _lru(proj, h0_ext, cw, cb, wbd, bias, lam, *, blocks_per_seq, tp, lead):
    rows = proj.shape[0]
    W = cw.shape[1]
    R = TIME_ROWS
    row_blk = lambda c: pl.BlockSpec((R, W), lambda i, c=c: (i, c))
    full = lambda a: pl.BlockSpec(a.shape, lambda i: (0,) * a.ndim)
    ins = [proj, proj]
    specs = [row_blk(0), row_blk(1)]
    if lead:
        ins.append(h0_ext)
        specs.append(row_blk(0))
    params = [cw, cb, wbd, bias, lam]
    ins += params
    specs += [full(p) for p in params]
    return pl.pallas_call(
        functools.partial(_lru_kernel, blocks_per_seq=blocks_per_seq, tp=tp, lead=lead),
        grid=(rows // R,),
        in_specs=specs,
        out_specs=[row_blk(0), row_blk(0)],
        out_shape=[jax.ShapeDtypeStruct((rows, W), F32)] * 2,
        scratch_shapes=[pltpu.VMEM((SUBLANES, W), F32), pltpu.VMEM((SUBLANES, W), F32),
                        pltpu.VMEM((R, W), F32), pltpu.VMEM((R, W), F32)],
        compiler_params=_cparams("arbitrary"),
        name="rg_lru",
    )(*ins)


def _prep_kernel(r_ref, k_ref, v_ref, lr_ref, mu_ref, mulr_ref, w0_ref, a0_ref,
                 w2_ref, a2_ref, g2_ref,
                 ro_ref, ko_ref, vo_ref, lw_ref, ah_ref, g_ref,
                 cr, ck, cv, clr, *, blocks_per_seq):
    i = pl.program_id(0)

    @pl.when(i % blocks_per_seq == 0)
    def _():
        for c in (cr, ck, cv, clr):
            c[...] = jnp.zeros_like(c)

    def token_shift(x_ref, c_ref, mu):
        x = x_ref[...]
        prev = _shift_rows(x, c_ref[...], 1)
        c_ref[...] = x[x.shape[0] - SUBLANES:]
        return x + (prev - x) * mu

    mu = mu_ref[...]
    ro_ref[...] = token_shift(r_ref, cr, mu[0:1])
    ko_ref[...] = token_shift(k_ref, ck, mu[1:2])
    vo_ref[...] = token_shift(v_ref, cv, mu[2:3])
    xl = token_shift(lr_ref, clr, mulr_ref[...])
    dw = jnp.dot(jnp.tanh(xl).astype(BF16), w2_ref[...], preferred_element_type=F32)
    w_log = -_softplus(-(w0_ref[...] + dw)) - 0.5
    lw_ref[...] = -jnp.exp(w_log)
    da = jnp.dot(xl.astype(BF16), a2_ref[...], preferred_element_type=F32)
    ah_ref[...] = _sigmoid(a0_ref[...] + da)
    g_ref[...] = jnp.dot(_sigmoid(xl).astype(BF16), g2_ref[...], preferred_element_type=F32)


def _prep(proj, proj_lr, mu, mulr, w0, a0, w2p, a2p, g2p, *, col0, blocks_per_seq):
    rows = proj.shape[0]
    W = w0.shape[1]
    LR = proj_lr.shape[1]
    R = TIME_ROWS
    row_blk = lambda c: pl.BlockSpec((R, W), lambda i, c=c: (i, c))
    full = lambda a: pl.BlockSpec(a.shape, lambda i: (0,) * a.ndim)
    params = [mu, mulr, w0, a0, w2p, a2p, g2p]
    return pl.pallas_call(
        functools.partial(_prep_kernel, blocks_per_seq=blocks_per_seq),
        grid=(rows // R,),
        in_specs=[row_blk(col0), row_blk(col0 + 1), row_blk(col0 + 2),
                  pl.BlockSpec((R, LR), lambda i: (i, 0))] + [full(p) for p in params],
        out_specs=[row_blk(0)] * 6,
        out_shape=[jax.ShapeDtypeStruct((rows, W), F32)] * 6,
        scratch_shapes=[pltpu.VMEM((SUBLANES, W), F32)] * 3 + [pltpu.VMEM((SUBLANES, LR), F32)],
        compiler_params=_cparams("arbitrary"),
        name="rwkv_prep",
    )(proj, proj, proj, proj_lr, *params)


def _dot(a, b):
    return jnp.dot(a, b, precision=HI, preferred_element_type=F32)


def _dot_nt(a, b):
    return lax.dot_general(a, b, (((1,), (1,)), ((), ())), precision=HI,
                           preferred_element_type=F32)


def _dot_tn(a, b):
    return lax.dot_general(a, b, (((0,), (0,)), ((), ())), precision=HI,
                           preferred_element_type=F32)


def _wkv_chunk(S0, r, lw, k, v, ah, kk_w, ka_w):
    C = r.shape[0]
    kk = k * kk_w
    nrm = jnp.sqrt(jnp.sum(kk * kk, axis=-1, keepdims=True))
    kk = kk / jnp.maximum(nrm, 1e-12)
    k_mod = k * (1.0 + (ah - 1.0) * ka_w)
    a = -kk
    b = kk * ah

    row = lax.broadcasted_iota(jnp.int32, (C, C), 0)
    col = lax.broadcasted_iota(jnp.int32, (C, C), 1)
    incl = col <= row
    strict = col < row
    cum = _dot(incl.astype(F32), lw)
    g_incl = jnp.exp(cum)
    g_excl = jnp.exp(cum - lw)
    g_inv = jnp.exp(-cum)
    rt = r * g_incl
    at = a * g_excl
    bt = b * g_inv
    kt = k_mod * g_inv

    a_ab = jnp.where(strict, _dot_nt(at, bt), 0.0)
    a_ak = jnp.where(strict, _dot_nt(at, kt), 0.0)
    a_rb = jnp.where(incl, _dot_nt(rt, bt), 0.0)
    a_rk = jnp.where(incl, _dot_nt(rt, kt), 0.0)

    u = _dot_nt(at, S0) + _dot(a_ak, v)
    p = a_ab
    n_iter = max(1, (C - 1).bit_length())
    for it in range(n_iter):
        u = u + _dot(p, u)
        if it + 1 < n_iter:
            p = _dot(p, p)

    y = _dot_nt(rt, S0) + _dot(a_rb, u) + _dot(a_rk, v)
    g_last = g_incl[C - 1:C]
    S1 = (S0 + _dot_tn(u, bt) + _dot_tn(v, kt)) * g_last
    return y, S1, k_mod


def _wkv_kernel(*refs, tp, chunk, lead, has_state):
    if has_state:
        (r_ref, k_ref, v_ref, lw_ref, ah_ref, g_ref, s0_ref,
         kk_ref, ka_ref, rk_ref, gg_ref, gb_ref, o_ref, so_ref, st) = refs
    else:
        (r_ref, k_ref, v_ref, lw_ref, ah_ref, g_ref,
         kk_ref, ka_ref, rk_ref, gg_ref, gb_ref, o_ref, so_ref, st) = refs
    N = HEAD_DIM
    heads = r_ref.shape[1] // N
    n_seq = r_ref.shape[0] // tp
    cps = tp // chunk

    def seq_body(s, _):
        if has_state:
            st[...] = s0_ref[s]
        else:
            st[...] = jnp.zeros_like(st)

        def chunk_body(c, _):
            off = pl.multiple_of(s * tp + c * chunk, SUBLANES)
            rows = pl.ds(off, chunk)
            outs = []
            for h in range(heads):
                lanes = slice(h * N, (h + 1) * N)
                r = r_ref[rows, lanes]
                k = k_ref[rows, lanes]
                v = v_ref[rows, lanes]
                lw = lw_ref[rows, lanes]
                ah = ah_ref[rows, lanes]
                if lead:
                    t = c * chunk + lax.broadcasted_iota(jnp.int32, (chunk, N), 0)
                    real = t >= lead
                    k = jnp.where(real, k, 0.0)
                    v = jnp.where(real, v, 0.0)
                    lw = jnp.where(real, lw, 0.0)
                y, s1, k_mod = _wkv_chunk(st[h], r, lw, k, v, ah,
                                          kk_ref[:, lanes], ka_ref[:, lanes])
                st[h] = s1
                mu = jnp.mean(y, axis=-1, keepdims=True)
                yc = y - mu
                var = jnp.mean(yc * yc, axis=-1, keepdims=True)
                yn = yc * lax.rsqrt(var + GN_EPS) * gg_ref[:, lanes] + gb_ref[:, lanes]
                bonus = jnp.sum(r * k_mod * rk_ref[:, lanes], axis=-1, keepdims=True) * v
                outs.append(yn + bonus)
            o_ref[rows, :] = jnp.concatenate(outs, axis=1) * g_ref[rows, :]
            return 0

        lax.fori_loop(0, cps, chunk_body, 0)
        so_ref[s] = st[...]
        return 0

    lax.fori_loop(0, n_seq, seq_body, 0)


def _wkv(r, k, v, lw, ah, g, s0, kk_w, ka_w, rk_w, gg_w, gb_w, *, n_seq, tp, seq_blk, chunk, lead):
    rows, W = r.shape
    N = HEAD_DIM
    H = W // N
    hp = LANES // N
    has_state = s0 is not None
    blk = pl.BlockSpec((seq_blk * tp, LANES), lambda i, p: (i, p))
    pblk = pl.BlockSpec((1, LANES), lambda i, p: (0, p))
    sblk = pl.BlockSpec((seq_blk, hp, N, N), lambda i, p: (i, p, 0, 0))
    ins = [r, k, v, lw, ah, g]
    specs = [blk] * 6
    if has_state:
        ins.append(s0)
        specs.append(sblk)
    ins += [kk_w, ka_w, rk_w, gg_w, gb_w]
    specs += [pblk] * 5
    return pl.pallas_call(
        functools.partial(_wkv_kernel, tp=tp, chunk=chunk, lead=lead, has_state=has_state),
        grid=(n_seq // seq_blk, H // hp),
        in_specs=specs,
        out_specs=[blk, sblk],
        out_shape=[jax.ShapeDtypeStruct((rows, W), F32),
                   jax.ShapeDtypeStruct((n_seq, H, N, N), F32)],
        scratch_shapes=[pltpu.VMEM((hp, N, N), F32)],
        compiler_params=_cparams("parallel", "parallel"),
        name="wkv7",
    )(*ins)


def _oproj_kernel(x_ref, a_ref, b_ref, w_ref, g_ref, bb_ref, o_ref, *, alpha):
    half = a_ref.shape[1]
    m = jnp.dot(a_ref[...].astype(BF16), w_ref[:half], preferred_element_type=F32)
    m = m + jnp.dot(b_ref[...].astype(BF16), w_ref[half:], preferred_element_type=F32)
    o_ref[...] = _layer_norm(alpha * x_ref[...] + m, g_ref[...], bb_ref[...])


def _oproj(x, lru_out, rwkv_out, w_o, g, b, *, alpha, tm):
    M, D = x.shape
    Wh = lru_out.shape[1]
    full = lambda a: pl.BlockSpec(a.shape, lambda i: (0,) * a.ndim)
    return pl.pallas_call(
        functools.partial(_oproj_kernel, alpha=alpha),
        grid=(M // tm,),
        in_specs=[pl.BlockSpec((tm, D), lambda i: (i, 0)),
                  pl.BlockSpec((tm, Wh), lambda i: (i, 0)),
                  pl.BlockSpec((tm, Wh), lambda i: (i, 0)),
                  full(w_o), full(g), full(b)],
        out_specs=pl.BlockSpec((tm, D), lambda i: (i, 0)),
        out_shape=jax.ShapeDtypeStruct((M, D), F32),
        compiler_params=_cparams("parallel"),
        name="out_proj_ln",
    )(x, lru_out, rwkv_out, w_o, g, b)


def _ffn_kernel(x_ref, wg_ref, wu_ref, wd_ref, g_ref, b_ref, o_ref, xb, acc, *, alpha):
    f = pl.program_id(1)

    @pl.when(f == 0)
    def _():
        xb[...] = x_ref[...].astype(BF16)
        acc[...] = jnp.zeros_like(acc)

    x = xb[...]
    hg = jnp.dot(x, wg_ref[...], preferred_element_type=F32)
    hu = jnp.dot(x, wu_ref[...], preferred_element_type=F32)
    h = (hg * _sigmoid(hg)) * hu
    acc[...] += jnp.dot(h.astype(BF16), wd_ref[...], preferred_element_type=F32)

    @pl.when(f == pl.num_programs(1) - 1)
    def _():
        o_ref[...] = _layer_norm(alpha * x_ref[...] + acc[...], g_ref[...], b_ref[...])


def _ffn(x, wg, wu, wd, g, b, *, alpha, tm, tf):
    M, D = x.shape
    F = wg.shape[1]
    full = lambda a: pl.BlockSpec(a.shape, lambda i, f: (0,) * a.ndim)
    return pl.pallas_call(
        functools.partial(_ffn_kernel, alpha=alpha),
        grid=(M // tm, F // tf),
        in_specs=[pl.BlockSpec((tm, D), lambda i, f: (i, 0)),
                  pl.BlockSpec((D, tf), lambda i, f: (0, f)),
                  pl.BlockSpec((D, tf), lambda i, f: (0, f)),
                  pl.BlockSpec((tf, D), lambda i, f: (f, 0)),
                  full(g), full(b)],
        out_specs=pl.BlockSpec((tm, D), lambda i, f: (i, 0)),
        out_shape=jax.ShapeDtypeStruct((M, D), F32),
        scratch_shapes=[pltpu.VMEM((tm, D), BF16), pltpu.VMEM((tm, D), F32)],
        compiler_params=_cparams("parallel", "arbitrary"),
        name="ffn_ln",
    )(x, wg, wu, wd, g, b)


def _router_kernel(x_ref, w_ref, o_ref, *, n_experts):
    logits = jnp.dot(x_ref[...], w_ref[...], precision=HI, preferred_element_type=F32)
    col = lax.broadcasted_iota(jnp.int32, logits.shape, 1)
    neg = jnp.float32(-jnp.inf)
    logits = jnp.where(col < n_experts, logits, neg)
    big = jnp.int32(logits.shape[1])
    m1 = jnp.max(logits, axis=-1, keepdims=True)
    i1 = jnp.min(jnp.where(logits == m1, col, big), axis=-1, keepdims=True)
    rest = jnp.where(col == i1, neg, logits)
    m2 = jnp.max(rest, axis=-1, keepdims=True)
    i2 = jnp.min(jnp.where(rest == m2, col, big), axis=-1, keepdims=True)
    e2 = jnp.exp(m2 - m1)
    p1 = 1.0 / (1.0 + e2)
    p2 = e2 / (1.0 + e2)
    o_ref[...] = jnp.where(col == i1, p1, 0.0) + jnp.where(col == i2, p2, 0.0)


def _router(x, w_router_p, *, n_experts, tm):
    M, D = x.shape
    NP = w_router_p.shape[1]
    return pl.pallas_call(
        functools.partial(_router_kernel, n_experts=n_experts),
        grid=(M // tm,),
        in_specs=[pl.BlockSpec((tm, D), lambda i: (i, 0)),
                  pl.BlockSpec((D, NP), lambda i: (0, 0))],
        out_specs=pl.BlockSpec((tm, NP), lambda i: (i, 0)),
        out_shape=jax.ShapeDtypeStruct((M, NP), F32),
        compiler_params=_cparams("parallel"),
        name="router_top2",
    )(x, w_router_p)


def _moe_kernel(x_ref, gates_ref, wg_ref, wu_ref, wd_ref, g_ref, b_ref, o_ref, xb, acc, *, alpha):
    e = pl.program_id(1)
    f = pl.program_id(2)

    @pl.when(jnp.logical_and(e == 0, f == 0))
    def _():
        xb[...] = x_ref[...].astype(BF16)
        acc[...] = jnp.zeros_like(acc)

    gates = gates_ref[...]
    col = lax.broadcasted_iota(jnp.int32, gates.shape, 1)
    ge = jnp.sum(jnp.where(col == e, gates, 0.0), axis=-1, keepdims=True)
    x = xb[...]
    hg = jnp.dot(x, wg_ref[...], preferred_element_type=F32)
    hu = jnp.dot(x, wu_ref[...], preferred_element_type=F32)
    h = (hg * _sigmoid(hg)) * hu * ge
    acc[...] += jnp.dot(h.astype(BF16), wd_ref[...], preferred_element_type=F32)

    @pl.when(jnp.logical_and(e == pl.num_programs(1) - 1, f == pl.num_programs(2) - 1))
    def _():
        o_ref[...] = _layer_norm(alpha * x_ref[...] + acc[...], g_ref[...], b_ref[...])


def _moe(x, gates, wg, wu, wd, g, b, *, alpha, tm, tf):
    M, D = x.shape
    E, _, F = wg.shape
    NP = gates.shape[1]
    full = lambda a: pl.BlockSpec(a.shape, lambda i, e, f: (0,) * a.ndim)
    return pl.pallas_call(
        functools.partial(_moe_kernel, alpha=alpha),
        grid=(M // tm, E, F // tf),
        in_specs=[pl.BlockSpec((tm, D), lambda i, e, f: (i, 0)),
                  pl.BlockSpec((tm, NP), lambda i, e, f: (i, 0)),
                  pl.BlockSpec((None, D, tf), lambda i, e, f: (e, 0, f)),
                  pl.BlockSpec((None, D, tf), lambda i, e, f: (e, 0, f)),
                  pl.BlockSpec((None, tf, D), lambda i, e, f: (e, f, 0)),
                  full(g), full(b)],
        out_specs=pl.BlockSpec((tm, D), lambda i, e, f: (i, 0)),
        out_shape=jax.ShapeDtypeStruct((M, D), F32),
        scratch_shapes=[pltpu.VMEM((tm, D), BF16), pltpu.VMEM((tm, D), F32)],
        compiler_params=_cparams("parallel", "arbitrary", "arbitrary"),
        name="moe_ln",
    )(x, gates, wg, wu, wd, g, b)


def _pad_to(a, axis, mult):
    n = a.shape[axis]
    pad = (-n) % mult
    if pad == 0:
        return a
    widths = [(0, 0)] * a.ndim
    widths[axis] = (0, pad)
    return jnp.pad(a, widths)


def _block_diag(w, group):
    n, d, e = w.shape
    wg = w.reshape(n // group, group, d, e)
    eye = jnp.eye(group, dtype=w.dtype)
    return jnp.einsum('gqde,qr->gqdre', wg, eye).reshape(n // group, group * d, group * e)


def kernel(x_prompt, x_sample, state_lru_conv, state_lru_h, state_rwkv_shift, state_rwkv_wkv,
           w_in, conv_w, conv_b, lru_wa, lru_ba, lru_wx, lru_bx, lru_lambda,
           mu_shift, w0, w2, a0, a2, g2, k_k, k_a, r_k, gn_g, gn_b, w_o,
           ln1_g, ln1_b, ln2_g, ln2_b, ffn_wg, ffn_wu, ffn_wd,
           w_router, exp_wg, exp_wu, exp_wd):
    Bp, Tp, D = x_prompt.shape
    Bs, Ts, _ = x_sample.shape
    depth = w_in.shape[0]
    W = conv_w.shape[2]
    n_rkv = 3 * W
    n_main = 2 * W + n_rkv
    n_lr = w_in.shape[2] - n_main
    r_decay, r_aaa = w2.shape[1], a2.shape[1]
    n_lr_p = n_lr + (-n_lr) % LANES
    alpha = (2.0 * depth) ** 0.25
    lead = SAMPLE_TP - Ts
    Mp, Ms = Bp * Tp, Bs * Ts
    group = 2 * LANES // HEAD_DIM

    row = lambda a: a.reshape(1, -1)
    x = jnp.concatenate([x_prompt.reshape(Mp, D), x_sample.reshape(Ms, D)], axis=0)
    outs = {k: [] for k in ("pc", "ph", "ps", "pw", "sc", "sh", "ss", "sw")}

    for l in range(depth):
        w_main = w_in[l, :, :n_main].astype(BF16)
        w_lr = _pad_to(w_in[l, :, n_main:], 1, LANES).astype(BF16)
        proj = _matmul(x, w_main, TOKEN_TILE, min(512, W))
        proj_lr = _matmul(x, w_lr, TOKEN_TILE, n_lr_p)

        st_main = jnp.zeros((Bs, lead, n_main), F32)
        st_main = st_main.at[:, lead - (CONV_W - 1):, :W].set(state_lru_conv[l])
        st_main = st_main.at[:, lead - 1, 2 * W:].set(state_rwkv_shift[l, :, :n_rkv])
        proj_s = jnp.concatenate([st_main, proj[Mp:].reshape(Bs, Ts, n_main)], axis=1)
        proj_s = proj_s.reshape(Bs * SAMPLE_TP, n_main)
        st_lr = jnp.zeros((Bs, lead, n_lr_p), F32)
        st_lr = st_lr.at[:, lead - 1, :n_lr].set(state_rwkv_shift[l, :, n_rkv:])
        lr_s = jnp.concatenate([st_lr, proj_lr[Mp:].reshape(Bs, Ts, n_lr_p)], axis=1)
        lr_s = lr_s.reshape(Bs * SAMPLE_TP, n_lr_p)
        h0_ext = jnp.zeros((Bs, SAMPLE_TP, W), F32).at[:, lead - 1].set(state_lru_h[l])
        h0_ext = h0_ext.reshape(Bs * SAMPLE_TP, W)
        proj_p = proj[:Mp]
        lr_p = proj_lr[:Mp]

        wbd = jnp.concatenate([_block_diag(lru_wa[l], group), _block_diag(lru_wx[l], group)],
                              axis=2).astype(BF16)
        bias = jnp.stack([lru_ba[l], lru_bx[l]])
        lru_args = (conv_w[l], row(conv_b[l]), wbd, bias, row(lru_lambda[l]))
        lru_p, h_p = _lru(proj_p, None, *lru_args, blocks_per_seq=Tp // TIME_ROWS, tp=Tp, lead=0)
        lru_s, h_s = _lru(proj_s, h0_ext, *lru_args, blocks_per_seq=1, tp=SAMPLE_TP, lead=lead)

        mu = mu_shift[l, :n_rkv].reshape(3, W)
        mulr = row(_pad_to(mu_shift[l, n_rkv:], 0, LANES))
        zeros_lr = lambda n: jnp.zeros((n, W), F32)
        w2p = jnp.concatenate([w2[l], zeros_lr(n_lr_p - r_decay)], axis=0).astype(BF16)
        a2p = jnp.concatenate([zeros_lr(r_decay), a2[l], zeros_lr(n_lr_p - r_decay - r_aaa)],
                              axis=0).astype(BF16)
        g2p = jnp.concatenate([zeros_lr(r_decay + r_aaa), g2[l], zeros_lr(n_lr_p - n_lr)],
                              axis=0).astype(BF16)
        prep_args = (mu, mulr, row(w0[l]), row(a0[l]), w2p, a2p, g2p)
        prep_p = _prep(proj_p, lr_p, *prep_args, col0=2, blocks_per_seq=Tp // TIME_ROWS)
        prep_s = _prep(proj_s, lr_s, *prep_args, col0=2, blocks_per_seq=1)
        head_args = (row(k_k[l]), row(k_a[l]), row(r_k[l]), row(gn_g[l]), row(gn_b[l]))
        rwkv_p, S_p = _wkv(*prep_p, None, *head_args, n_seq=Bp, tp=Tp, seq_blk=1,
                           chunk=64, lead=0)
        rwkv_s, S_s = _wkv(*prep_s, state_rwkv_wkv[l], *head_args, n_seq=Bs, tp=SAMPLE_TP,
                           seq_blk=16, chunk=SAMPLE_TP, lead=lead)

        real = lambda a: a.reshape(Bs, SAMPLE_TP, -1)[:, lead:].reshape(Ms, -1)
        lru_out = jnp.concatenate([lru_p, real(lru_s)], axis=0)
        rwkv_out = jnp.concatenate([rwkv_p, real(rwkv_s)], axis=0)
        x = _oproj(x, lru_out, rwkv_out, w_o[l].astype(BF16), row(ln1_g[l]), row(ln1_b[l]),
                   alpha=alpha, tm=256)

        pm = proj_p.reshape(Bp, Tp, n_main)
        outs["pc"].append(pm[:, Tp - (CONV_W - 1):, :W])
        outs["ph"].append(h_p.reshape(Bp, Tp, W)[:, -1])
        outs["ps"].append(jnp.concatenate([pm[:, -1, 2 * W:],
                                           lr_p.reshape(Bp, Tp, n_lr_p)[:, -1, :n_lr]], axis=-1))
        outs["pw"].append(S_p)
        sm = proj_s.reshape(Bs, SAMPLE_TP, n_main)
        outs["sc"].append(sm[:, SAMPLE_TP - (CONV_W - 1):, :W])
        outs["sh"].append(h_s.reshape(Bs, SAMPLE_TP, W)[:, -1])
        outs["ss"].append(jnp.concatenate([sm[:, -1, 2 * W:],
                                           lr_s.reshape(Bs, SAMPLE_TP, n_lr_p)[:, -1, :n_lr]],
                                          axis=-1))
        outs["sw"].append(S_s)

        j = l // 2
        if l % 2 == 0:
            wg = _pad_to(ffn_wg[j], 1, FF_TILE).astype(BF16)
            wu = _pad_to(ffn_wu[j], 1, FF_TILE).astype(BF16)
            wd = _pad_to(ffn_wd[j], 0, FF_TILE).astype(BF16)
            x = _ffn(x, wg, wu, wd, row(ln2_g[l]), row(ln2_b[l]),
                     alpha=alpha, tm=TOKEN_TILE, tf=FF_TILE)
        else:
            gates = _router(x, _pad_to(w_router[j], 1, LANES), n_experts=N_EXPERTS, tm=TOKEN_TILE)
            wg = _pad_to(exp_wg[j], 2, FF_TILE).astype(BF16)
            wu = _pad_to(exp_wu[j], 2, FF_TILE).astype(BF16)
            wd = _pad_to(exp_wd[j], 1, FF_TILE).astype(BF16)
            x = _moe(x, gates, wg, wu, wd, row(ln2_g[l]), row(ln2_b[l]),
                     alpha=alpha, tm=TOKEN_TILE, tf=FF_TILE)

    y_prompt = x[:Mp].reshape(Bp, Tp, D)
    y_sample = x[Mp:].reshape(Bs, Ts, D)
    st = lambda k: jnp.stack(outs[k])
    return (y_prompt, y_sample, st("pc"), st("ph"), st("ps"), st("pw"),
            st("sc"), st("sh"), st("ss"), st("sw"))
```

```python
import functools

import jax
import jax.numpy as jnp
from jax import lax
from jax.experimental import pallas as pl
from jax.experimental.pallas import tpu as pltpu

F32 = jnp.float32
BF16 = jnp.bfloat16
HI = lax.Precision.HIGHEST

HEAD_DIM = 64
CONV_W = 4
LRU_C = 8.0
N_EXPERTS = 8
LN_EPS = 1e-5
GN_EPS = 64e-5

LANES = 128
SUBLANES = 8
VMEM_LIMIT = 48 * 1024 * 1024
SAMPLE_TP = 8
TIME_ROWS = 256
TOKEN_TILE = 512
FF_TILE = 512
WKV_CHUNK = 64
WKV_ROWS = 512
WKV_WIDTH = 512
PRE_CHUNKS = 2
WKV_SAMPLE_ROWS = 128
WKV_SAMPLE_WIDTH = 256
WKV_SAMPLE_GROUP = 8


def _cparams(*sem):
    return pltpu.CompilerParams(dimension_semantics=sem, vmem_limit_bytes=VMEM_LIMIT)


def _sigmoid(x):
    return 1.0 / (1.0 + jnp.exp(-x))


def _softplus(x):
    return jnp.maximum(x, 0.0) + jnp.log1p(jnp.exp(-jnp.abs(x)))


def _gelu_tanh(x):
    c = 0.7978845608028654
    return 0.5 * x * (1.0 + jnp.tanh(c * (x + 0.044715 * (x * x * x))))


def _layer_norm(x, g, b):
    mu = jnp.mean(x, axis=-1, keepdims=True)
    xc = x - mu
    var = jnp.mean(xc * xc, axis=-1, keepdims=True)
    return xc * lax.rsqrt(var + LN_EPS) * g + b


def _shift_rows(x, carry, k):
    xr = pltpu.roll(x, k, 0)
    cr = pltpu.roll(carry, k, 0)
    row = lax.broadcasted_iota(jnp.int32, carry.shape, 0)
    first = jnp.where(row < k, cr, xr[:SUBLANES])
    return jnp.concatenate([first, xr[SUBLANES:]], axis=0)


def _mm_kernel(x_ref, w_ref, o_ref):
    o_ref[...] = jnp.dot(x_ref[...].astype(BF16), w_ref[...], preferred_element_type=F32)


def _matmul(x, w, tm, tn):
    M, K = x.shape
    N = w.shape[1]
    return pl.pallas_call(
        _mm_kernel,
        grid=(M // tm, N // tn),
        in_specs=[pl.BlockSpec((tm, K), lambda i, j: (i, 0)),
                  pl.BlockSpec((K, tn), lambda i, j: (0, j))],
        out_specs=pl.BlockSpec((tm, tn), lambda i, j: (i, j)),
        out_shape=jax.ShapeDtypeStruct((M, N), F32),
        compiler_params=_cparams("parallel", "parallel"),
        name="in_proj",
    )(x, w)


def _lru_kernel(*refs, blocks_per_seq, tp, lead):
    if lead:
        (xb_ref, gate_ref, h0_ref, cw_ref, cb_ref, wbd_ref, bias_ref, lam_ref,
         out_ref, h_ref, xcarry, hcarry, a_s, u_s) = refs
    else:
        (xb_ref, gate_ref, cw_ref, cb_ref, wbd_ref, bias_ref, lam_ref,
         out_ref, h_ref, xcarry, hcarry, a_s, u_s) = refs
    R, W = xb_ref.shape
    i = pl.program_id(0)

    @pl.when(i % blocks_per_seq == 0)
    def _():
        xcarry[...] = jnp.zeros_like(xcarry)
        hcarry[...] = jnp.zeros_like(hcarry)

    xb = xb_ref[...]
    carry = xcarry[...]
    cw = cw_ref[...]
    xc = cb_ref[...] + cw[3:4] * xb
    for k in range(1, CONV_W):
        xc = xc + cw[3 - k:4 - k] * _shift_rows(xb, carry, k)
    xcarry[...] = xb[R - SUBLANES:]

    xcb = xc.astype(BF16)
    nb = wbd_ref.shape[0]
    bw = W // nb
    pre_a, pre_x = [], []
    for g in range(nb):
        pre = jnp.dot(xcb[:, g * bw:(g + 1) * bw], wbd_ref[g], preferred_element_type=F32)
        pre_a.append(pre[:, :bw])
        pre_x.append(pre[:, bw:])
    bias = bias_ref[...]
    r = _sigmoid(jnp.concatenate(pre_a, axis=1) + bias[0:1])
    ig = _sigmoid(jnp.concatenate(pre_x, axis=1) + bias[1:2])
    log_a = -LRU_C * r * _softplus(-lam_ref[...])
    a = jnp.exp(log_a)
    u = jnp.sqrt(-jnp.tanh(log_a) * (a * a + 1.0)) * (ig * xc)
    if lead:
        row = lax.broadcasted_iota(jnp.int32, (R, W), 0)
        is_state = (row % tp) == (lead - 1)
        a = jnp.where(is_state, 0.0, a)
        u = jnp.where(is_state, h0_ref[...], u)
    a_s[...] = a
    u_s[...] = u

    def tile_body(t, h):
        off = pl.multiple_of(t * SUBLANES, SUBLANES)
        a8 = a_s[pl.ds(off, SUBLANES), :]
        u8 = u_s[pl.ds(off, SUBLANES), :]
        rows = []
        for j in range(SUBLANES):
            h = a8[j:j + 1] * h + u8[j:j + 1]
            rows.append(h)
        h_ref[pl.ds(off, SUBLANES), :] = jnp.concatenate(rows, axis=0)
        return h

    h_last = lax.fori_loop(0, R // SUBLANES, tile_body, hcarry[0:1])
    hcarry[...] = jnp.broadcast_to(h_last, hcarry.shape)
    out_ref[...] = h_ref[...] * _gelu_tanh(gate_ref[...])


def _lru(proj, h0_ext, cw, cb, wbd, bias, lam, *, blocks_per_seq, tp, lead):
    rows = proj.shape[0]
    W = cw.shape[1]
    R = TIME_ROWS
    row_blk = lambda c: pl.BlockSpec((R, W), lambda i, c=c: (i, c))
    full = lambda a: pl.BlockSpec(a.shape, lambda i: (0,) * a.ndim)
    ins = [proj, proj]
    specs = [row_blk(0), row_blk(1)]
    if lead:
        ins.append(h0_ext)
        specs.append(row_blk(0))
    params = [cw, cb, wbd, bias, lam]
    ins += params
    specs += [full(p) for p in params]
    return pl.pallas_call(
        functools.partial(_lru_kernel, blocks_per_seq=blocks_per_seq, tp=tp, lead=lead),
        grid=(rows // R,),
        in_specs=specs,
        out_specs=[row_blk(0), row_blk(0)],
        out_shape=[jax.ShapeDtypeStruct((rows, W), F32)] * 2,
        scratch_shapes=[pltpu.VMEM((SUBLANES, W), F32), pltpu.VMEM((SUBLANES, W), F32),
                        pltpu.VMEM((R, W), F32), pltpu.VMEM((R, W), F32)],
        compiler_params=_cparams("arbitrary"),
        name="rg_lru",
    )(*ins)


def _prep_kernel(r_ref, k_ref, v_ref, lr_ref, mu_ref, mulr_ref, w0_ref, a0_ref, kkw_ref, kaw_ref,
                 w2_ref, a2_ref, g2_ref, ones_ref,
                 ro_ref, km_ref, vo_ref, lw_ref, cum_ref, kk_ref, bk_ref, g_ref,
                 cr, ck, cv, clr, *, blocks_per_seq, tp, lead, chunk):
    i = pl.program_id(0)

    @pl.when(i % blocks_per_seq == 0)
    def _():
        for c in (cr, ck, cv, clr):
            c[...] = jnp.zeros_like(c)

    def token_shift(x_ref, c_ref, mu):
        x = x_ref[...]
        prev = _shift_rows(x, c_ref[...], 1)
        c_ref[...] = x[x.shape[0] - SUBLANES:]
        return x + (prev - x) * mu

    mu = mu_ref[...]
    r = token_shift(r_ref, cr, mu[0:1])
    k = token_shift(k_ref, ck, mu[1:2])
    v = token_shift(v_ref, cv, mu[2:3])
    xl = token_shift(lr_ref, clr, mulr_ref[...])
    dw = jnp.dot(jnp.tanh(xl).astype(BF16), w2_ref[...], preferred_element_type=F32)
    w_log = -_softplus(-(w0_ref[...] + dw)) - 0.5
    lw = -jnp.exp(w_log)
    da = jnp.dot(xl.astype(BF16), a2_ref[...], preferred_element_type=F32)
    ah = _sigmoid(a0_ref[...] + da)
    g_ref[...] = jnp.dot(_sigmoid(xl).astype(BF16), g2_ref[...], preferred_element_type=F32)

    R = r.shape[0]
    if lead:
        t = lax.broadcasted_iota(jnp.int32, r.shape, 0) % tp
        real = t >= lead
        k = jnp.where(real, k, 0.0)
        v = jnp.where(real, v, 0.0)
        lw = jnp.where(real, lw, 0.0)

    kk = k * kkw_ref[...]
    sq = kk * kk
    sq_hi = sq.astype(BF16)
    sq_lo = (sq - sq_hi.astype(F32)).astype(BF16)
    ss = (jnp.dot(sq_hi, ones_ref[...], preferred_element_type=F32)
          + jnp.dot(sq_lo, ones_ref[...], preferred_element_type=F32))
    kk = kk / jnp.maximum(jnp.sqrt(ss), 1e-12)

    shift = chunk.bit_length() - 1
    rr = lax.broadcasted_iota(jnp.int32, (R, R), 0)
    cc = lax.broadcasted_iota(jnp.int32, (R, R), 1)
    same = jnp.right_shift(rr, shift) == jnp.right_shift(cc, shift)
    tri = jnp.where(jnp.logical_and(same, cc <= rr), 1.0, 0.0)
    cum_ref[...] = jnp.dot(tri, lw, precision=HI, preferred_element_type=F32)

    ro_ref[...] = r
    km_ref[...] = k * (1.0 + (ah - 1.0) * kaw_ref[...])
    vo_ref[...] = v
    lw_ref[...] = lw
    kk_ref[...] = kk
    bk_ref[...] = kk * ah


def _prep(proj, proj_lr, params, *, col0, blocks_per_seq, tp, lead, chunk):
    rows = proj.shape[0]
    W = params[2].shape[1]
    LR = proj_lr.shape[1]
    R = TIME_ROWS
    row_blk = lambda c: pl.BlockSpec((R, W), lambda i, c=c: (i, c))
    full = lambda a: pl.BlockSpec(a.shape, lambda i: (0,) * a.ndim)
    return pl.pallas_call(
        functools.partial(_prep_kernel, blocks_per_seq=blocks_per_seq, tp=tp, lead=lead,
                          chunk=chunk),
        grid=(rows // R,),
        in_specs=[row_blk(col0), row_blk(col0 + 1), row_blk(col0 + 2),
                  pl.BlockSpec((R, LR), lambda i: (i, 0))] + [full(p) for p in params],
        out_specs=[row_blk(0)] * 8,
        out_shape=[jax.ShapeDtypeStruct((rows, W), F32)] * 8,
        scratch_shapes=[pltpu.VMEM((SUBLANES, W), F32)] * 3 + [pltpu.VMEM((SUBLANES, LR), F32)],
        compiler_params=_cparams("arbitrary"),
        name="rwkv_prep",
    )(proj, proj, proj, proj_lr, *params)


def _mm(a, b):
    return jnp.dot(a.astype(BF16), b.astype(BF16), preferred_element_type=F32)


def _mm_nt(a, b):
    return lax.dot_general(a.astype(BF16), b.astype(BF16), (((1,), (1,)), ((), ())),
                           preferred_element_type=F32)


def _mm_tn(a, b):
    return lax.dot_general(a.astype(BF16), b.astype(BF16), (((0,), (0,)), ((), ())),
                           preferred_element_type=F32)


def _pairs_pre(tiles):
    C = tiles[0][0].shape[0]
    N = HEAD_DIM
    n = range(len(tiles))
    h0 = lax.broadcasted_iota(jnp.int32, (C, LANES), 1) < N
    stack = lambda x: jnp.concatenate([jnp.where(h0, x, 0.0), jnp.where(h0, 0.0, x)], axis=0)
    ri = lax.broadcasted_iota(jnp.int32, (2 * C, 2 * C), 0)
    ci = lax.broadcasted_iota(jnp.int32, (2 * C, 2 * C), 1)
    strict = (ci & (C - 1)) < (ri & (C - 1))
    incl = (ci & (C - 1)) <= (ri & (C - 1))
    eye = jnp.where(ri == ci, 1.0, 0.0)

    lhs, rhs, vs, zs, gl = [], [], [], [], []
    for r, kmod, v, lw, cum, kk, bk in tiles:
        g_incl = jnp.exp(cum)
        g_excl = jnp.exp(cum - lw)
        g_inv = jnp.exp(-cum)
        g_rest = jnp.exp(cum[C - 1:C] - cum)
        lhs.append(jnp.concatenate([stack(-(kk * g_excl)), stack(r * g_incl)],
                                   axis=0).astype(BF16))
        rhs.append(jnp.concatenate([stack(bk * g_inv), stack(kmod * g_inv)],
                                   axis=0).astype(BF16))
        vs.append(stack(v).astype(BF16))
        zs.append(jnp.concatenate([stack(bk * g_rest), stack(kmod * g_rest)],
                                  axis=0).astype(BF16))
        gl.append(g_incl[C - 1:C])

    aa = [_mm_nt(lhs[i], rhs[i]) for i in n]
    a_ab = [jnp.where(strict, aa[i][:2 * C, :2 * C], 0.0) for i in n]
    a_ak = [jnp.where(strict, aa[i][:2 * C, 2 * C:], 0.0).astype(BF16) for i in n]
    arbk = [jnp.concatenate([jnp.where(incl, aa[i][2 * C:, :2 * C], 0.0),
                             jnp.where(incl, aa[i][2 * C:, 2 * C:], 0.0)],
                            axis=1).astype(BF16) for i in n]
    av = [_mm(a_ak[i], vs[i]) for i in n]

    t = [eye + a_ab[i] for i in n]
    n_iter = max(1, (C - 1).bit_length())
    if n_iter > 1:
        p = [_mm(a_ab[i], a_ab[i]) for i in n]
    for it in range(1, n_iter):
        if it + 1 < n_iter:
            pt = [_mm(p[i], jnp.concatenate([t[i], p[i]], axis=1)) for i in n]
            t = [t[i] + pt[i][:, :2 * C] for i in n]
            p = [pt[i][:, 2 * C:] for i in n]
        else:
            t = [t[i] + _mm(p[i], t[i]) for i in n]
    t = [t[i].astype(BF16) for i in n]
    tav = [_mm(t[i], av[i]) for i in n]
    return [(lhs[i], t[i], tav[i], arbk[i], vs[i], zs[i], gl[i]) for i in n]


def _pairs_step(states, pres):
    n = range(len(states))
    C2 = pres[0][1].shape[0]
    ls = [_mm_nt(pres[i][0], states[i]) for i in n]
    u = [_mm(pres[i][1], ls[i][:C2]) + pres[i][2] for i in n]
    xs = [jnp.concatenate([u[i].astype(BF16), pres[i][4]], axis=0) for i in n]
    s1 = [states[i] * pres[i][6] + _mm_tn(xs[i], pres[i][5]) for i in n]
    ys = [ls[i][C2:] + _mm(pres[i][3], xs[i]) for i in n]
    return [(ys[i][:C2 // 2] + ys[i][C2 // 2:], s1[i]) for i in n]


def _wkv_epilogue(y, r, kmod, v, g, rk, gg, gb):
    N = HEAD_DIM
    h0 = lax.broadcasted_iota(jnp.int32, y.shape, 1) < N

    def head_sum(x):
        s0 = jnp.sum(jnp.where(h0, x, 0.0), axis=-1, keepdims=True)
        s1 = jnp.sum(jnp.where(h0, 0.0, x), axis=-1, keepdims=True)
        return jnp.where(h0, s0, s1)

    yc = y - head_sum(y) * (1.0 / N)
    var = head_sum(yc * yc) * (1.0 / N)
    yn = yc * lax.rsqrt(var + GN_EPS) * gg + gb
    bonus = head_sum(r * kmod * rk) * v
    return (yn + bonus) * g


def _wkv_prompt_kernel(r_ref, km_ref, v_ref, lw_ref, cum_ref, kk_ref, bk_ref, g_ref,
                       rk_ref, gg_ref, gb_ref, o_ref, so_ref,
                       st, lhs_s, t_s, tav_s, arbk_s, vs_s, zs_s, gl_s, *, chunk):
    N = HEAD_DIM
    C = chunk
    rows_blk, width = r_ref.shape
    pairs = width // LANES
    n_chunks = rows_blk // C
    tb = pl.program_id(2)

    @pl.when(tb == 0)
    def _():
        st[...] = jnp.zeros_like(st)

    in_refs = (r_ref, km_ref, v_ref, lw_ref, cum_ref, kk_ref, bk_ref)
    pre_refs = (lhs_s, t_s, tav_s, arbk_s, vs_s, zs_s)
    lanes = [slice(p * LANES, (p + 1) * LANES) for p in range(pairs)]

    def pre_body(c2, _):
        where = [(c2 * PRE_CHUNKS + j, p) for j in range(PRE_CHUNKS) for p in range(pairs)]
        tiles = []
        for c, p in where:
            rows = pl.ds(pl.multiple_of(c * C, C), C)
            tiles.append(tuple(ref[rows, lanes[p]] for ref in in_refs))
        for (c, p), pre in zip(where, _pairs_pre(tiles)):
            for ref, val in zip(pre_refs, pre[:6]):
                ref[c, p] = val
            gl_s[c, p] = jnp.broadcast_to(pre[6], (SUBLANES, LANES))
        return 0

    lax.fori_loop(0, n_chunks // PRE_CHUNKS, pre_body, 0)

    def step_body(c, _):
        rows = pl.ds(pl.multiple_of(c * C, C), C)
        pres = [tuple(ref[c, p] for ref in pre_refs) + (gl_s[c, p][0:1],) for p in range(pairs)]
        outs = _pairs_step([st[p] for p in range(pairs)], pres)
        for p, (y, s1) in enumerate(outs):
            st[p] = s1
            o_ref[rows, lanes[p]] = _wkv_epilogue(
                y, r_ref[rows, lanes[p]], km_ref[rows, lanes[p]], v_ref[rows, lanes[p]],
                g_ref[rows, lanes[p]], rk_ref[:, lanes[p]], gg_ref[:, lanes[p]],
                gb_ref[:, lanes[p]])
        return 0

    lax.fori_loop(0, n_chunks, step_body, 0)

    @pl.when(tb == pl.num_programs(2) - 1)
    def _():
        for p in range(pairs):
            sb = st[p]
            so_ref[0, 2 * p] = sb[:N, :N]
            so_ref[0, 2 * p + 1] = sb[N:, N:]


def _wkv_sample_kernel(r_ref, km_ref, v_ref, lw_ref, cum_ref, kk_ref, bk_ref, g_ref, s0_ref,
                       rk_ref, gg_ref, gb_ref, o_ref, so_ref, *, tp, group):
    N = HEAD_DIM
    rows_blk, width = r_ref.shape
    pairs = width // LANES
    n_seq = rows_blk // tp
    zero = jnp.zeros((N, N), F32)

    in_refs = (r_ref, km_ref, v_ref, lw_ref, cum_ref, kk_ref, bk_ref)
    lanes = [slice(p * LANES, (p + 1) * LANES) for p in range(pairs)]

    def group_body(gi, _):
        where = [(gi * group + j, p) for j in range(group) for p in range(pairs)]
        tiles, states = [], []
        for s, p in where:
            rows = pl.ds(pl.multiple_of(s * tp, SUBLANES), tp)
            tiles.append(tuple(ref[rows, lanes[p]] for ref in in_refs))
            top = jnp.concatenate([s0_ref[s, 2 * p], zero], axis=1)
            bot = jnp.concatenate([zero, s0_ref[s, 2 * p + 1]], axis=1)
            states.append(jnp.concatenate([top, bot], axis=0))
        outs = _pairs_step(states, _pairs_pre(tiles))
        for (s, p), tile, (y, s1) in zip(where, tiles, outs):
            rows = pl.ds(pl.multiple_of(s * tp, SUBLANES), tp)
            so_ref[s, 2 * p] = s1[:N, :N]
            so_ref[s, 2 * p + 1] = s1[N:, N:]
            o_ref[rows, lanes[p]] = _wkv_epilogue(
                y, tile[0], tile[1], tile[2], g_ref[rows, lanes[p]],
                rk_ref[:, lanes[p]], gg_ref[:, lanes[p]], gb_ref[:, lanes[p]])
        return 0

    lax.fori_loop(0, n_seq // group, group_body, 0)


def _wkv_prompt(ops, rk_w, gg_w, gb_w, *, n_seq, tp, rows_blk, chunk, width):
    rows, W = ops[0].shape
    N = HEAD_DIM
    H = W // N
    hq = width // N
    pairs = width // LANES
    tbs = tp // rows_blk
    n_chunks = rows_blk // chunk
    C2, C4 = 2 * chunk, 4 * chunk
    blk = pl.BlockSpec((rows_blk, width), lambda i, q, t: (i * tbs + t, q))
    pblk = pl.BlockSpec((1, width), lambda i, q, t: (0, q))
    sblk = pl.BlockSpec((1, hq, N, N), lambda i, q, t: (i, q, 0, 0))
    per_chunk = lambda shape, dt: pltpu.VMEM((n_chunks, pairs) + shape, dt)
    return pl.pallas_call(
        functools.partial(_wkv_prompt_kernel, chunk=chunk),
        grid=(n_seq, H // hq, tbs),
        in_specs=[blk] * len(ops) + [pblk] * 3,
        out_specs=[blk, sblk],
        out_shape=[jax.ShapeDtypeStruct((rows, W), F32),
                   jax.ShapeDtypeStruct((n_seq, H, N, N), F32)],
        scratch_shapes=[pltpu.VMEM((pairs, LANES, LANES), F32),
                        per_chunk((C4, LANES), BF16), per_chunk((C2, C2), BF16),
                        per_chunk((C2, LANES), F32), per_chunk((C2, C4), BF16),
                        per_chunk((C2, LANES), BF16), per_chunk((C4, LANES), BF16),
                        per_chunk((SUBLANES, LANES), F32)],
        compiler_params=_cparams("parallel", "parallel", "arbitrary"),
        name="wkv7_prompt",
    )(*ops, rk_w, gg_w, gb_w)


def _wkv_sample(ops, s0, rk_w, gg_w, gb_w, *, n_seq, tp, rows_blk, width, group):
    rows, W = ops[0].shape
    N = HEAD_DIM
    H = W // N
    hq = width // N
    seq_blk = rows_blk // tp
    blk = pl.BlockSpec((rows_blk, width), lambda i, q: (i, q))
    pblk = pl.BlockSpec((1, width), lambda i, q: (0, q))
    sblk = pl.BlockSpec((seq_blk, hq, N, N), lambda i, q: (i, q, 0, 0))
    return pl.pallas_call(
        functools.partial(_wkv_sample_kernel, tp=tp, group=group),
        grid=(n_seq // seq_blk, H // hq),
        in_specs=[blk] * len(ops) + [sblk] + [pblk] * 3,
        out_specs=[blk, sblk],
        out_shape=[jax.ShapeDtypeStruct((rows, W), F32),
                   jax.ShapeDtypeStruct((n_seq, H, N, N), F32)],
        compiler_params=_cparams("parallel", "parallel"),
        name="wkv7_sample",
    )(*ops, s0, rk_w, gg_w, gb_w)


def _oproj_kernel(x_ref, a_ref, b_ref, w_ref, g_ref, bb_ref, o_ref, *, alpha):
    half = a_ref.shape[1]
    m = jnp.dot(a_ref[...].astype(BF16), w_ref[:half], preferred_element_type=F32)
    m = m + jnp.dot(b_ref[...].astype(BF16), w_ref[half:], preferred_element_type=F32)
    o_ref[...] = _layer_norm(alpha * x_ref[...] + m, g_ref[...], bb_ref[...])


def _oproj(x, lru_out, rwkv_out, w_o, g, b, *, alpha, tm):
    M, D = x.shape
    Wh = lru_out.shape[1]
    full = lambda a: pl.BlockSpec(a.shape, lambda i: (0,) * a.ndim)
    return pl.pallas_call(
        functools.partial(_oproj_kernel, alpha=alpha),
        grid=(M // tm,),
        in_specs=[pl.BlockSpec((tm, D), lambda i: (i, 0)),
                  pl.BlockSpec((tm, Wh), lambda i: (i, 0)),
                  pl.BlockSpec((tm, Wh), lambda i: (i, 0)),
                  full(w_o), full(g), full(b)],
        out_specs=pl.BlockSpec((tm, D), lambda i: (i, 0)),
        out_shape=jax.ShapeDtypeStruct((M, D), F32),
        compiler_params=_cparams("parallel"),
        name="out_proj_ln",
    )(x, lru_out, rwkv_out, w_o, g, b)


def _ffn_kernel(x_ref, wg_ref, wu_ref, wd_ref, g_ref, b_ref, o_ref, xb, acc, *, alpha):
    f = pl.program_id(1)

    @pl.when(f == 0)
    def _():
        xb[...] = x_ref[...].astype(BF16)
        acc[...] = jnp.zeros_like(acc)

    x = xb[...]
    hg = jnp.dot(x, wg_ref[...], preferred_element_type=F32)
    hu = jnp.dot(x, wu_ref[...], preferred_element_type=F32)
    h = (hg * _sigmoid(hg)) * hu
    acc[...] += jnp.dot(h.astype(BF16), wd_ref[...], preferred_element_type=F32)

    @pl.when(f == pl.num_programs(1) - 1)
    def _():
        o_ref[...] = _layer_norm(alpha * x_ref[...] + acc[...], g_ref[...], b_ref[...])


def _ffn(x, wg, wu, wd, g, b, *, alpha, tm, tf):
    M, D = x.shape
    F = wg.shape[1]
    full = lambda a: pl.BlockSpec(a.shape, lambda i, f: (0,) * a.ndim)
    return pl.pallas_call(
        functools.partial(_ffn_kernel, alpha=alpha),
        grid=(M // tm, F // tf),
        in_specs=[pl.BlockSpec((tm, D), lambda i, f: (i, 0)),
                  pl.BlockSpec((D, tf), lambda i, f: (0, f)),
                  pl.BlockSpec((D, tf), lambda i, f: (0, f)),
                  pl.BlockSpec((tf, D), lambda i, f: (f, 0)),
                  full(g), full(b)],
        out_specs=pl.BlockSpec((tm, D), lambda i, f: (i, 0)),
        out_shape=jax.ShapeDtypeStruct((M, D), F32),
        scratch_shapes=[pltpu.VMEM((tm, D), BF16), pltpu.VMEM((tm, D), F32)],
        compiler_params=_cparams("parallel", "arbitrary"),
        name="ffn_ln",
    )(x, wg, wu, wd, g, b)


def _router_kernel(x_ref, w_ref, o_ref, *, n_experts):
    logits = jnp.dot(x_ref[...], w_ref[...], precision=HI, preferred_element_type=F32)
    col = lax.broadcasted_iota(jnp.int32, logits.shape, 1)
    neg = jnp.float32(-jnp.inf)
    logits = jnp.where(col < n_experts, logits, neg)
    big = jnp.int32(logits.shape[1])
    m1 = jnp.max(logits, axis=-1, keepdims=True)
    i1 = jnp.min(jnp.where(logits == m1, col, big), axis=-1, keepdims=True)
    rest = jnp.where(col == i1, neg, logits)
    m2 = jnp.max(rest, axis=-1, keepdims=True)
    i2 = jnp.min(jnp.where(rest == m2, col, big), axis=-1, keepdims=True)
    e2 = jnp.exp(m2 - m1)
    p1 = 1.0 / (1.0 + e2)
    p2 = e2 / (1.0 + e2)
    o_ref[...] = jnp.where(col == i1, p1, 0.0) + jnp.where(col == i2, p2, 0.0)


def _router(x, w_router_p, *, n_experts, tm):
    M, D = x.shape
    NP = w_router_p.shape[1]
    return pl.pallas_call(
        functools.partial(_router_kernel, n_experts=n_experts),
        grid=(M // tm,),
        in_specs=[pl.BlockSpec((tm, D), lambda i: (i, 0)),
                  pl.BlockSpec((D, NP), lambda i: (0, 0))],
        out_specs=pl.BlockSpec((tm, NP), lambda i: (i, 0)),
        out_shape=jax.ShapeDtypeStruct((M, NP), F32),
        compiler_params=_cparams("parallel"),
        name="router_top2",
    )(x, w_router_p)


def _moe_kernel(x_ref, gates_ref, wg_ref, wu_ref, wd_ref, g_ref, b_ref, o_ref, xb, acc, *, alpha):
    e = pl.program_id(1)
    f = pl.program_id(2)

    @pl.when(jnp.logical_and(e == 0, f == 0))
    def _():
        xb[...] = x_ref[...].astype(BF16)
        acc[...] = jnp.zeros_like(acc)

    gates = gates_ref[...]
    col = lax.broadcasted_iota(jnp.int32, gates.shape, 1)
    ge = jnp.sum(jnp.where(col == e, gates, 0.0), axis=-1, keepdims=True)
    x = xb[...]
    hg = jnp.dot(x, wg_ref[...], preferred_element_type=F32)
    hu = jnp.dot(x, wu_ref[...], preferred_element_type=F32)
    h = (hg * _sigmoid(hg)) * hu * ge
    acc[...] += jnp.dot(h.astype(BF16), wd_ref[...], preferred_element_type=F32)

    @pl.when(jnp.logical_and(e == pl.num_programs(1) - 1, f == pl.num_programs(2) - 1))
    def _():
        o_ref[...] = _layer_norm(alpha * x_ref[...] + acc[...], g_ref[...], b_ref[...])


def _moe(x, gates, wg, wu, wd, g, b, *, alpha, tm, tf):
    M, D = x.shape
    E, _, F = wg.shape
    NP = gates.shape[1]
    full = lambda a: pl.BlockSpec(a.shape, lambda i, e, f: (0,) * a.ndim)
    return pl.pallas_call(
        functools.partial(_moe_kernel, alpha=alpha),
        grid=(M // tm, E, F // tf),
        in_specs=[pl.BlockSpec((tm, D), lambda i, e, f: (i, 0)),
                  pl.BlockSpec((tm, NP), lambda i, e, f: (i, 0)),
                  pl.BlockSpec((None, D, tf), lambda i, e, f: (e, 0, f)),
                  pl.BlockSpec((None, D, tf), lambda i, e, f: (e, 0, f)),
                  pl.BlockSpec((None, tf, D), lambda i, e, f: (e, f, 0)),
                  full(g), full(b)],
        out_specs=pl.BlockSpec((tm, D), lambda i, e, f: (i, 0)),
        out_shape=jax.ShapeDtypeStruct((M, D), F32),
        scratch_shapes=[pltpu.VMEM((tm, D), BF16), pltpu.VMEM((tm, D), F32)],
        compiler_params=_cparams("parallel", "arbitrary", "arbitrary"),
        name="moe_ln",
    )(x, gates, wg, wu, wd, g, b)


def _pad_to(a, axis, mult):
    n = a.shape[axis]
    pad = (-n) % mult
    if pad == 0:
        return a
    widths = [(0, 0)] * a.ndim
    widths[axis] = (0, pad)
    return jnp.pad(a, widths)


def _block_diag(w, group):
    n, d, e = w.shape
    wg = w.reshape(n // group, group, d, e)
    eye = jnp.eye(group, dtype=w.dtype)
    return jnp.einsum('gqde,qr->gqdre', wg, eye).reshape(n // group, group * d, group * e)


def kernel(x_prompt, x_sample, state_lru_conv, state_lru_h, state_rwkv_shift, state_rwkv_wkv,
           w_in, conv_w, conv_b, lru_wa, lru_ba, lru_wx, lru_bx, lru_lambda,
           mu_shift, w0, w2, a0, a2, g2, k_k, k_a, r_k, gn_g, gn_b, w_o,
           ln1_g, ln1_b, ln2_g, ln2_b, ffn_wg, ffn_wu, ffn_wd,
           w_router, exp_wg, exp_wu, exp_wd):
    Bp, Tp, D = x_prompt.shape
    Bs, Ts, _ = x_sample.shape
    depth = w_in.shape[0]
    W = conv_w.shape[2]
    n_rkv = 3 * W
    n_main = 2 * W + n_rkv
    n_lr = w_in.shape[2] - n_main
    r_decay, r_aaa = w2.shape[1], a2.shape[1]
    n_lr_p = n_lr + (-n_lr) % LANES
    alpha = (2.0 * depth) ** 0.25
    lead = SAMPLE_TP - Ts
    Mp, Ms = Bp * Tp, Bs * Ts
    group = 2 * LANES // HEAD_DIM

    row = lambda a: a.reshape(1, -1)
    head_ones = jnp.kron(jnp.eye(W // HEAD_DIM, dtype=F32),
                         jnp.ones((HEAD_DIM, HEAD_DIM), F32)).astype(BF16)
    x = jnp.concatenate([x_prompt.reshape(Mp, D), x_sample.reshape(Ms, D)], axis=0)
    outs = {k: [] for k in ("pc", "ph", "ps", "pw", "sc", "sh", "ss", "sw")}

    for l in range(depth):
        w_main = w_in[l, :, :n_main].astype(BF16)
        w_lr = _pad_to(w_in[l, :, n_main:], 1, LANES).astype(BF16)
        proj = _matmul(x, w_main, TOKEN_TILE, min(512, W))
        proj_lr = _matmul(x, w_lr, TOKEN_TILE, n_lr_p)

        st_main = jnp.zeros((Bs, lead, n_main), F32)
        st_main = st_main.at[:, lead - (CONV_W - 1):, :W].set(state_lru_conv[l])
        st_main = st_main.at[:, lead - 1, 2 * W:].set(state_rwkv_shift[l, :, :n_rkv])
        proj_s = jnp.concatenate([st_main, proj[Mp:].reshape(Bs, Ts, n_main)], axis=1)
        proj_s = proj_s.reshape(Bs * SAMPLE_TP, n_main)
        st_lr = jnp.zeros((Bs, lead, n_lr_p), F32)
        st_lr = st_lr.at[:, lead - 1, :n_lr].set(state_rwkv_shift[l, :, n_rkv:])
        lr_s = jnp.concatenate([st_lr, proj_lr[Mp:].reshape(Bs, Ts, n_lr_p)], axis=1)
        lr_s = lr_s.reshape(Bs * SAMPLE_TP, n_lr_p)
        h0_ext = jnp.zeros((Bs, SAMPLE_TP, W), F32).at[:, lead - 1].set(state_lru_h[l])
        h0_ext = h0_ext.reshape(Bs * SAMPLE_TP, W)
        proj_p = proj[:Mp]
        lr_p = proj_lr[:Mp]

        wbd = jnp.concatenate([_block_diag(lru_wa[l], group), _block_diag(lru_wx[l], group)],
                              axis=2).astype(BF16)
        bias = jnp.stack([lru_ba[l], lru_bx[l]])
        lru_args = (conv_w[l], row(conv_b[l]), wbd, bias, row(lru_lambda[l]))
        lru_p, h_p = _lru(proj_p, None, *lru_args, blocks_per_seq=Tp // TIME_ROWS, tp=Tp, lead=0)
        lru_s, h_s = _lru(proj_s, h0_ext, *lru_args, blocks_per_seq=1, tp=SAMPLE_TP, lead=lead)

        mu = mu_shift[l, :n_rkv].reshape(3, W)
        mulr = row(_pad_to(mu_shift[l, n_rkv:], 0, LANES))
        zeros_lr = lambda n: jnp.zeros((n, W), F32)
        w2p = jnp.concatenate([w2[l], zeros_lr(n_lr_p - r_decay)], axis=0).astype(BF16)
        a2p = jnp.concatenate([zeros_lr(r_decay), a2[l], zeros_lr(n_lr_p - r_decay - r_aaa)],
                              axis=0).astype(BF16)
        g2p = jnp.concatenate([zeros_lr(r_decay + r_aaa), g2[l], zeros_lr(n_lr_p - n_lr)],
                              axis=0).astype(BF16)
        prep_params = (mu, mulr, row(w0[l]), row(a0[l]), row(k_k[l]), row(k_a[l]),
                       w2p, a2p, g2p, head_ones)
        ops_p = _prep(proj_p, lr_p, prep_params, col0=2, blocks_per_seq=Tp // TIME_ROWS,
                      tp=Tp, lead=0, chunk=WKV_CHUNK)
        ops_s = _prep(proj_s, lr_s, prep_params, col0=2, blocks_per_seq=1,
                      tp=SAMPLE_TP, lead=lead, chunk=SAMPLE_TP)
        head_args = (row(r_k[l]), row(gn_g[l]), row(gn_b[l]))
        rwkv_p, S_p = _wkv_prompt(ops_p, *head_args, n_seq=Bp, tp=Tp, rows_blk=WKV_ROWS,
                                  chunk=WKV_CHUNK, width=min(W, WKV_WIDTH))
        rwkv_s, S_s = _wkv_sample(ops_s, state_rwkv_wkv[l], *head_args, n_seq=Bs, tp=SAMPLE_TP,
                                  rows_blk=WKV_SAMPLE_ROWS, width=min(W, WKV_SAMPLE_WIDTH),
                                  group=WKV_SAMPLE_GROUP)

        real = lambda a: a.reshape(Bs, SAMPLE_TP, -1)[:, lead:].reshape(Ms, -1)
        lru_out = jnp.concatenate([lru_p, real(lru_s)], axis=0)
        rwkv_out = jnp.concatenate([rwkv_p, real(rwkv_s)], axis=0)
        x = _oproj(x, lru_out, rwkv_out, w_o[l].astype(BF16), row(ln1_g[l]), row(ln1_b[l]),
                   alpha=alpha, tm=256)

        pm = proj_p.reshape(Bp, Tp, n_main)
        outs["pc"].append(pm[:, Tp - (CONV_W - 1):, :W])
        outs["ph"].append(h_p.reshape(Bp, Tp, W)[:, -1])
        outs["ps"].append(jnp.concatenate([pm[:, -1, 2 * W:],
                                           lr_p.reshape(Bp, Tp, n_lr_p)[:, -1, :n_lr]], axis=-1))
        outs["pw"].append(S_p)
        sm = proj_s.reshape(Bs, SAMPLE_TP, n_main)
        outs["sc"].append(sm[:, SAMPLE_TP - (CONV_W - 1):, :W])
        outs["sh"].append(h_s.reshape(Bs, SAMPLE_TP, W)[:, -1])
        outs["ss"].append(jnp.concatenate([sm[:, -1, 2 * W:],
                                           lr_s.reshape(Bs, SAMPLE_TP, n_lr_p)[:, -1, :n_lr]],
                                          axis=-1))
        outs["sw"].append(S_s)

        j = l // 2
        if l % 2 == 0:
            wg = _pad_to(ffn_wg[j], 1, FF_TILE).astype(BF16)
            wu = _pad_to(ffn_wu[j], 1, FF_TILE).astype(BF16)
            wd = _pad_to(ffn_wd[j], 0, FF_TILE).astype(BF16)
            x = _ffn(x, wg, wu, wd, row(ln2_g[l]), row(ln2_b[l]),
                     alpha=alpha, tm=TOKEN_TILE, tf=FF_TILE)
        else:
            gates = _router(x, _pad_to(w_router[j], 1, LANES), n_experts=N_EXPERTS, tm=TOKEN_TILE)
            wg = _pad_to(exp_wg[j], 2, FF_TILE).astype(BF16)
            wu = _pad_to(exp_wu[j], 2, FF_TILE).astype(BF16)
            wd = _pad_to(exp_wd[j], 1, FF_TILE).astype(BF16)
            x = _moe(x, gates, wg, wu, wd, row(ln2_g[l]), row(ln2_b[l]),
                     alpha=alpha, tm=TOKEN_TILE, tf=FF_TILE)

    y_prompt = x[:Mp].reshape(Bp, Tp, D)
    y_sample = x[Mp:].reshape(Bs, Ts, D)
    st = lambda k: jnp.stack(outs[k])
    return (y_prompt, y_sample, st("pc"), st("ph"), st("ps"), st("pw"),
            st("sc"), st("sh"), st("ss"), st("sw"))
```

```python
import functools

import jax
import jax.numpy as jnp
from jax import lax
from jax.experimental import pallas as pl
from jax.experimental.pallas import tpu as pltpu

F32 = jnp.float32
BF16 = jnp.bfloat16
HI = lax.Precision.HIGHEST

HEAD_DIM = 64
CONV_W = 4
LRU_C = 8.0
N_EXPERTS = 8
LN_EPS = 1e-5
GN_EPS = 64e-5

LANES = 128
SUBLANES = 8
VMEM_LIMIT = 48 * 1024 * 1024
SAMPLE_TP = 8
TIME_ROWS = 256
TOKEN_TILE = 512
FF_TILE = 512
EXPERT_TILE = 768
EXPERT_VMEM_LIMIT = 56 * 1024 * 1024
WKV_CHUNK = 64
WKV_ROWS = 512
WKV_WIDTH = 512
PRE_CHUNKS = 2
WKV_SAMPLE_ROWS = 128
WKV_SAMPLE_WIDTH = 256
WKV_SAMPLE_GROUP = 8


def _cparams(*sem):
    return pltpu.CompilerParams(dimension_semantics=sem, vmem_limit_bytes=VMEM_LIMIT)


def _sigmoid(x):
    return 1.0 / (1.0 + jnp.exp(-x))


def _softplus(x):
    return jnp.maximum(x, 0.0) + jnp.log1p(jnp.exp(-jnp.abs(x)))


def _gelu_tanh(x):
    c = 0.7978845608028654
    return 0.5 * x * (1.0 + jnp.tanh(c * (x + 0.044715 * (x * x * x))))


def _layer_norm(x, g, b):
    mu = jnp.mean(x, axis=-1, keepdims=True)
    xc = x - mu
    var = jnp.mean(xc * xc, axis=-1, keepdims=True)
    return xc * lax.rsqrt(var + LN_EPS) * g + b


def _shift_rows(x, carry, k):
    xr = pltpu.roll(x, k, 0)
    cr = pltpu.roll(carry, k, 0)
    row = lax.broadcasted_iota(jnp.int32, carry.shape, 0)
    first = jnp.where(row < k, cr, xr[:SUBLANES])
    return jnp.concatenate([first, xr[SUBLANES:]], axis=0)


def _mm_kernel(x_ref, w_ref, o_ref):
    o_ref[...] = jnp.dot(x_ref[...].astype(BF16), w_ref[...], preferred_element_type=F32)


def _matmul(x, w, tm, tn):
    M, K = x.shape
    N = w.shape[1]
    return pl.pallas_call(
        _mm_kernel,
        grid=(M // tm, N // tn),
        in_specs=[pl.BlockSpec((tm, K), lambda i, j: (i, 0)),
                  pl.BlockSpec((K, tn), lambda i, j: (0, j))],
        out_specs=pl.BlockSpec((tm, tn), lambda i, j: (i, j)),
        out_shape=jax.ShapeDtypeStruct((M, N), F32),
        compiler_params=_cparams("parallel", "parallel"),
        name="in_proj",
    )(x, w)


def _lru_kernel(*refs, blocks_per_seq, tp, lead):
    if lead:
        (xb_ref, gate_ref, h0_ref, cw_ref, cb_ref, wbd_ref, bias_ref, lam_ref,
         out_ref, h_ref, xcarry, hcarry, a_s, u_s) = refs
    else:
        (xb_ref, gate_ref, cw_ref, cb_ref, wbd_ref, bias_ref, lam_ref,
         out_ref, h_ref, xcarry, hcarry, a_s, u_s) = refs
    R, W = xb_ref.shape
    i = pl.program_id(0)

    @pl.when(i % blocks_per_seq == 0)
    def _():
        xcarry[...] = jnp.zeros_like(xcarry)
        hcarry[...] = jnp.zeros_like(hcarry)

    xb = xb_ref[...]
    carry = xcarry[...]
    cw = cw_ref[...]
    xc = cb_ref[...] + cw[3:4] * xb
    for k in range(1, CONV_W):
        xc = xc + cw[3 - k:4 - k] * _shift_rows(xb, carry, k)
    xcarry[...] = xb[R - SUBLANES:]

    xcb = xc.astype(BF16)
    nb = wbd_ref.shape[0]
    bw = W // nb
    pre_a, pre_x = [], []
    for g in range(nb):
        pre = jnp.dot(xcb[:, g * bw:(g + 1) * bw], wbd_ref[g], preferred_element_type=F32)
        pre_a.append(pre[:, :bw])
        pre_x.append(pre[:, bw:])
    bias = bias_ref[...]
    r = _sigmoid(jnp.concatenate(pre_a, axis=1) + bias[0:1])
    ig = _sigmoid(jnp.concatenate(pre_x, axis=1) + bias[1:2])
    log_a = -LRU_C * r * _softplus(-lam_ref[...])
    a = jnp.exp(log_a)
    u = jnp.sqrt(-jnp.tanh(log_a) * (a * a + 1.0)) * (ig * xc)
    if lead:
        row = lax.broadcasted_iota(jnp.int32, (R, W), 0)
        is_state = (row % tp) == (lead - 1)
        a = jnp.where(is_state, 0.0, a)
        u = jnp.where(is_state, h0_ref[...], u)
    a_s[...] = a
    u_s[...] = u

    def tile_body(t, h):
        off = pl.multiple_of(t * SUBLANES, SUBLANES)
        a8 = a_s[pl.ds(off, SUBLANES), :]
        u8 = u_s[pl.ds(off, SUBLANES), :]
        rows = []
        for j in range(SUBLANES):
            h = a8[j:j + 1] * h + u8[j:j + 1]
            rows.append(h)
        h_ref[pl.ds(off, SUBLANES), :] = jnp.concatenate(rows, axis=0)
        return h

    h_last = lax.fori_loop(0, R // SUBLANES, tile_body, hcarry[0:1])
    hcarry[...] = jnp.broadcast_to(h_last, hcarry.shape)
    out_ref[...] = h_ref[...] * _gelu_tanh(gate_ref[...])


def _lru(proj, h0_ext, cw, cb, wbd, bias, lam, *, rows, blocks_per_seq, tp, lead):
    W = cw.shape[1]
    R = TIME_ROWS
    row_blk = lambda c: pl.BlockSpec((R, W), lambda i, c=c: (i, c))
    full = lambda a: pl.BlockSpec(a.shape, lambda i: (0,) * a.ndim)
    ins = [proj, proj]
    specs = [row_blk(0), row_blk(1)]
    if lead:
        ins.append(h0_ext)
        specs.append(row_blk(0))
    params = [cw, cb, wbd, bias, lam]
    ins += params
    specs += [full(p) for p in params]
    return pl.pallas_call(
        functools.partial(_lru_kernel, blocks_per_seq=blocks_per_seq, tp=tp, lead=lead),
        grid=(rows // R,),
        in_specs=specs,
        out_specs=[row_blk(0), row_blk(0)],
        out_shape=[jax.ShapeDtypeStruct((rows, W), F32)] * 2,
        scratch_shapes=[pltpu.VMEM((SUBLANES, W), F32), pltpu.VMEM((SUBLANES, W), F32),
                        pltpu.VMEM((R, W), F32), pltpu.VMEM((R, W), F32)],
        compiler_params=_cparams("arbitrary"),
        name="rg_lru",
    )(*ins)


def _prep_kernel(r_ref, k_ref, v_ref, lr_ref, mu_ref, mulr_ref, w0_ref, a0_ref, kkw_ref, kaw_ref,
                 w2_ref, a2_ref, g2_ref, ones_ref,
                 ro_ref, km_ref, vo_ref, lw_ref, cum_ref, kk_ref, bk_ref, g_ref,
                 cr, ck, cv, clr, *, blocks_per_seq, tp, lead, chunk):
    i = pl.program_id(0)

    @pl.when(i % blocks_per_seq == 0)
    def _():
        for c in (cr, ck, cv, clr):
            c[...] = jnp.zeros_like(c)

    def token_shift(x_ref, c_ref, mu):
        x = x_ref[...]
        prev = _shift_rows(x, c_ref[...], 1)
        c_ref[...] = x[x.shape[0] - SUBLANES:]
        return x + (prev - x) * mu

    mu = mu_ref[...]
    r = token_shift(r_ref, cr, mu[0:1])
    k = token_shift(k_ref, ck, mu[1:2])
    v = token_shift(v_ref, cv, mu[2:3])
    xl = token_shift(lr_ref, clr, mulr_ref[...])
    dw = jnp.dot(jnp.tanh(xl).astype(BF16), w2_ref[...], preferred_element_type=F32)
    w_log = -_softplus(-(w0_ref[...] + dw)) - 0.5
    lw = -jnp.exp(w_log)
    da = jnp.dot(xl.astype(BF16), a2_ref[...], preferred_element_type=F32)
    ah = _sigmoid(a0_ref[...] + da)
    g_ref[...] = jnp.dot(_sigmoid(xl).astype(BF16), g2_ref[...], preferred_element_type=F32)

    R = r.shape[0]
    if lead:
        t = lax.broadcasted_iota(jnp.int32, r.shape, 0) % tp
        real = t >= lead
        k = jnp.where(real, k, 0.0)
        v = jnp.where(real, v, 0.0)
        lw = jnp.where(real, lw, 0.0)

    kk = k * kkw_ref[...]
    sq = kk * kk
    sq_hi = sq.astype(BF16)
    sq_lo = (sq - sq_hi.astype(F32)).astype(BF16)
    ss = (jnp.dot(sq_hi, ones_ref[...], preferred_element_type=F32)
          + jnp.dot(sq_lo, ones_ref[...], preferred_element_type=F32))
    kk = kk / jnp.maximum(jnp.sqrt(ss), 1e-12)

    shift = chunk.bit_length() - 1
    rr = lax.broadcasted_iota(jnp.int32, (R, R), 0)
    cc = lax.broadcasted_iota(jnp.int32, (R, R), 1)
    same = jnp.right_shift(rr, shift) == jnp.right_shift(cc, shift)
    tri = jnp.where(jnp.logical_and(same, cc <= rr), 1.0, 0.0)
    cum_ref[...] = jnp.dot(tri, lw, precision=HI, preferred_element_type=F32)

    ro_ref[...] = r
    km_ref[...] = k * (1.0 + (ah - 1.0) * kaw_ref[...])
    vo_ref[...] = v
    lw_ref[...] = lw
    kk_ref[...] = kk
    bk_ref[...] = kk * ah


def _prep(proj, proj_lr, params, *, rows, col0, blocks_per_seq, tp, lead, chunk):
    W = params[2].shape[1]
    LR = proj_lr.shape[1]
    R = TIME_ROWS
    row_blk = lambda c: pl.BlockSpec((R, W), lambda i, c=c: (i, c))
    full = lambda a: pl.BlockSpec(a.shape, lambda i: (0,) * a.ndim)
    return pl.pallas_call(
        functools.partial(_prep_kernel, blocks_per_seq=blocks_per_seq, tp=tp, lead=lead,
                          chunk=chunk),
        grid=(rows // R,),
        in_specs=[row_blk(col0), row_blk(col0 + 1), row_blk(col0 + 2),
                  pl.BlockSpec((R, LR), lambda i: (i, 0))] + [full(p) for p in params],
        out_specs=[row_blk(0)] * 8,
        out_shape=[jax.ShapeDtypeStruct((rows, W), F32)] * 8,
        scratch_shapes=[pltpu.VMEM((SUBLANES, W), F32)] * 3 + [pltpu.VMEM((SUBLANES, LR), F32)],
        compiler_params=_cparams("arbitrary"),
        name="rwkv_prep",
    )(proj, proj, proj, proj_lr, *params)


def _mm(a, b):
    return jnp.dot(a.astype(BF16), b.astype(BF16), preferred_element_type=F32)


def _mm_nt(a, b):
    return lax.dot_general(a.astype(BF16), b.astype(BF16), (((1,), (1,)), ((), ())),
                           preferred_element_type=F32)


def _mm_tn(a, b):
    return lax.dot_general(a.astype(BF16), b.astype(BF16), (((0,), (0,)), ((), ())),
                           preferred_element_type=F32)


def _pairs_pre(tiles):
    C = tiles[0][0].shape[0]
    N = HEAD_DIM
    n = range(len(tiles))
    h0 = lax.broadcasted_iota(jnp.int32, (C, LANES), 1) < N
    stack = lambda x: jnp.concatenate([jnp.where(h0, x, 0.0), jnp.where(h0, 0.0, x)], axis=0)
    ri = lax.broadcasted_iota(jnp.int32, (2 * C, 2 * C), 0)
    ci = lax.broadcasted_iota(jnp.int32, (2 * C, 2 * C), 1)
    strict = (ci & (C - 1)) < (ri & (C - 1))
    incl = (ci & (C - 1)) <= (ri & (C - 1))
    eye = jnp.where(ri == ci, 1.0, 0.0)

    lhs, rhs, vs, zs, gl = [], [], [], [], []
    for r, kmod, v, lw, cum, kk, bk in tiles:
        g_incl = jnp.exp(cum)
        g_excl = jnp.exp(cum - lw)
        g_inv = jnp.exp(-cum)
        g_rest = jnp.exp(cum[C - 1:C] - cum)
        lhs.append(jnp.concatenate([stack(-(kk * g_excl)), stack(r * g_incl)],
                                   axis=0).astype(BF16))
        rhs.append(jnp.concatenate([stack(bk * g_inv), stack(kmod * g_inv)],
                                   axis=0).astype(BF16))
        vs.append(stack(v).astype(BF16))
        zs.append(jnp.concatenate([stack(bk * g_rest), stack(kmod * g_rest)],
                                  axis=0).astype(BF16))
        gl.append(g_incl[C - 1:C])

    aa = [_mm_nt(lhs[i], rhs[i]) for i in n]
    a_ab = [jnp.where(strict, aa[i][:2 * C, :2 * C], 0.0) for i in n]
    a_ak = [jnp.where(strict, aa[i][:2 * C, 2 * C:], 0.0).astype(BF16) for i in n]
    arbk = [jnp.concatenate([jnp.where(incl, aa[i][2 * C:, :2 * C], 0.0),
                             jnp.where(incl, aa[i][2 * C:, 2 * C:], 0.0)],
                            axis=1).astype(BF16) for i in n]
    av = [_mm(a_ak[i], vs[i]) for i in n]

    t = [eye + a_ab[i] for i in n]
    n_iter = max(1, (C - 1).bit_length())
    if n_iter > 1:
        p = [_mm(a_ab[i], a_ab[i]) for i in n]
    for it in range(1, n_iter):
        if it + 1 < n_iter:
            pt = [_mm(p[i], jnp.concatenate([t[i], p[i]], axis=1)) for i in n]
            t = [t[i] + pt[i][:, :2 * C] for i in n]
            p = [pt[i][:, 2 * C:] for i in n]
        else:
            t = [t[i] + _mm(p[i], t[i]) for i in n]
    t = [t[i].astype(BF16) for i in n]
    tav = [_mm(t[i], av[i]) for i in n]
    return [(lhs[i], t[i], tav[i], arbk[i], vs[i], zs[i], gl[i]) for i in n]


def _pairs_step(states, pres):
    n = range(len(states))
    C2 = pres[0][1].shape[0]
    ls = [_mm_nt(pres[i][0], states[i]) for i in n]
    u = [_mm(pres[i][1], ls[i][:C2]) + pres[i][2] for i in n]
    xs = [jnp.concatenate([u[i].astype(BF16), pres[i][4]], axis=0) for i in n]
    s1 = [states[i] * pres[i][6] + _mm_tn(xs[i], pres[i][5]) for i in n]
    ys = [ls[i][C2:] + _mm(pres[i][3], xs[i]) for i in n]
    return [(ys[i][:C2 // 2] + ys[i][C2 // 2:], s1[i]) for i in n]


def _wkv_epilogue(y, r, kmod, v, g, rk, gg, gb):
    N = HEAD_DIM
    h0 = lax.broadcasted_iota(jnp.int32, y.shape, 1) < N

    def head_sum(x):
        s0 = jnp.sum(jnp.where(h0, x, 0.0), axis=-1, keepdims=True)
        s1 = jnp.sum(jnp.where(h0, 0.0, x), axis=-1, keepdims=True)
        return jnp.where(h0, s0, s1)

    yc = y - head_sum(y) * (1.0 / N)
    var = head_sum(yc * yc) * (1.0 / N)
    yn = yc * lax.rsqrt(var + GN_EPS) * gg + gb
    bonus = head_sum(r * kmod * rk) * v
    return (yn + bonus) * g


def _wkv_prompt_kernel(r_ref, km_ref, v_ref, lw_ref, cum_ref, kk_ref, bk_ref, g_ref,
                       rk_ref, gg_ref, gb_ref, o_ref, so_ref,
                       st, lhs_s, t_s, tav_s, arbk_s, vs_s, zs_s, gl_s, *, chunk):
    N = HEAD_DIM
    C = chunk
    rows_blk, width = r_ref.shape
    pairs = width // LANES
    n_chunks = rows_blk // C
    tb = pl.program_id(2)

    @pl.when(tb == 0)
    def _():
        st[...] = jnp.zeros_like(st)

    in_refs = (r_ref, km_ref, v_ref, lw_ref, cum_ref, kk_ref, bk_ref)
    pre_refs = (lhs_s, t_s, tav_s, arbk_s, vs_s, zs_s)
    lanes = [slice(p * LANES, (p + 1) * LANES) for p in range(pairs)]

    def pre_body(c2, _):
        where = [(c2 * PRE_CHUNKS + j, p) for j in range(PRE_CHUNKS) for p in range(pairs)]
        tiles = []
        for c, p in where:
            rows = pl.ds(pl.multiple_of(c * C, C), C)
            tiles.append(tuple(ref[rows, lanes[p]] for ref in in_refs))
        for (c, p), pre in zip(where, _pairs_pre(tiles)):
            for ref, val in zip(pre_refs, pre[:6]):
                ref[c, p] = val
            gl_s[c, p] = jnp.broadcast_to(pre[6], (SUBLANES, LANES))
        return 0

    lax.fori_loop(0, n_chunks // PRE_CHUNKS, pre_body, 0)

    def step_body(c, _):
        rows = pl.ds(pl.multiple_of(c * C, C), C)
        pres = [tuple(ref[c, p] for ref in pre_refs) + (gl_s[c, p][0:1],) for p in range(pairs)]
        outs = _pairs_step([st[p] for p in range(pairs)], pres)
        for p, (y, s1) in enumerate(outs):
            st[p] = s1
            o_ref[rows, lanes[p]] = _wkv_epilogue(
                y, r_ref[rows, lanes[p]], km_ref[rows, lanes[p]], v_ref[rows, lanes[p]],
                g_ref[rows, lanes[p]], rk_ref[:, lanes[p]], gg_ref[:, lanes[p]],
                gb_ref[:, lanes[p]])
        return 0

    lax.fori_loop(0, n_chunks, step_body, 0)

    @pl.when(tb == pl.num_programs(2) - 1)
    def _():
        for p in range(pairs):
            sb = st[p]
            so_ref[0, 2 * p] = sb[:N, :N]
            so_ref[0, 2 * p + 1] = sb[N:, N:]


def _wkv_sample_kernel(r_ref, km_ref, v_ref, lw_ref, cum_ref, kk_ref, bk_ref, g_ref, s0_ref,
                       rk_ref, gg_ref, gb_ref, o_ref, so_ref, *, tp, group):
    N = HEAD_DIM
    rows_blk, width = r_ref.shape
    pairs = width // LANES
    n_seq = rows_blk // tp
    zero = jnp.zeros((N, N), F32)

    in_refs = (r_ref, km_ref, v_ref, lw_ref, cum_ref, kk_ref, bk_ref)
    lanes = [slice(p * LANES, (p + 1) * LANES) for p in range(pairs)]

    def group_body(gi, _):
        where = [(gi * group + j, p) for j in range(group) for p in range(pairs)]
        tiles, states = [], []
        for s, p in where:
            rows = pl.ds(pl.multiple_of(s * tp, SUBLANES), tp)
            tiles.append(tuple(ref[rows, lanes[p]] for ref in in_refs))
            top = jnp.concatenate([s0_ref[s, 2 * p], zero], axis=1)
            bot = jnp.concatenate([zero, s0_ref[s, 2 * p + 1]], axis=1)
            states.append(jnp.concatenate([top, bot], axis=0))
        outs = _pairs_step(states, _pairs_pre(tiles))
        for (s, p), tile, (y, s1) in zip(where, tiles, outs):
            rows = pl.ds(pl.multiple_of(s * tp, SUBLANES), tp)
            so_ref[s, 2 * p] = s1[:N, :N]
            so_ref[s, 2 * p + 1] = s1[N:, N:]
            o_ref[rows, lanes[p]] = _wkv_epilogue(
                y, tile[0], tile[1], tile[2], g_ref[rows, lanes[p]],
                rk_ref[:, lanes[p]], gg_ref[:, lanes[p]], gb_ref[:, lanes[p]])
        return 0

    lax.fori_loop(0, n_seq // group, group_body, 0)


def _wkv_prompt(ops, rk_w, gg_w, gb_w, *, n_seq, tp, rows_blk, chunk, width):
    rows, W = ops[0].shape
    N = HEAD_DIM
    H = W // N
    hq = width // N
    pairs = width // LANES
    tbs = tp // rows_blk
    n_chunks = rows_blk // chunk
    C2, C4 = 2 * chunk, 4 * chunk
    blk = pl.BlockSpec((rows_blk, width), lambda i, q, t: (i * tbs + t, q))
    pblk = pl.BlockSpec((1, width), lambda i, q, t: (0, q))
    sblk = pl.BlockSpec((1, hq, N, N), lambda i, q, t: (i, q, 0, 0))
    per_chunk = lambda shape, dt: pltpu.VMEM((n_chunks, pairs) + shape, dt)
    return pl.pallas_call(
        functools.partial(_wkv_prompt_kernel, chunk=chunk),
        grid=(n_seq, H // hq, tbs),
        in_specs=[blk] * len(ops) + [pblk] * 3,
        out_specs=[blk, sblk],
        out_shape=[jax.ShapeDtypeStruct((rows, W), F32),
                   jax.ShapeDtypeStruct((n_seq, H, N, N), F32)],
        scratch_shapes=[pltpu.VMEM((pairs, LANES, LANES), F32),
                        per_chunk((C4, LANES), BF16), per_chunk((C2, C2), BF16),
                        per_chunk((C2, LANES), F32), per_chunk((C2, C4), BF16),
                        per_chunk((C2, LANES), BF16), per_chunk((C4, LANES), BF16),
                        per_chunk((SUBLANES, LANES), F32)],
        compiler_params=_cparams("parallel", "parallel", "arbitrary"),
        name="wkv7_prompt",
    )(*ops, rk_w, gg_w, gb_w)


def _wkv_sample(ops, s0, rk_w, gg_w, gb_w, *, n_seq, tp, rows_blk, width, group):
    rows, W = ops[0].shape
    N = HEAD_DIM
    H = W // N
    hq = width // N
    seq_blk = rows_blk // tp
    blk = pl.BlockSpec((rows_blk, width), lambda i, q: (i, q))
    pblk = pl.BlockSpec((1, width), lambda i, q: (0, q))
    sblk = pl.BlockSpec((seq_blk, hq, N, N), lambda i, q: (i, q, 0, 0))
    return pl.pallas_call(
        functools.partial(_wkv_sample_kernel, tp=tp, group=group),
        grid=(n_seq // seq_blk, H // hq),
        in_specs=[blk] * len(ops) + [sblk] + [pblk] * 3,
        out_specs=[blk, sblk],
        out_shape=[jax.ShapeDtypeStruct((rows, W), F32),
                   jax.ShapeDtypeStruct((n_seq, H, N, N), F32)],
        compiler_params=_cparams("parallel", "parallel"),
        name="wkv7_sample",
    )(*ops, s0, rk_w, gg_w, gb_w)


def _oproj_kernel(x_ref, a_ref, b_ref, w_ref, g_ref, bb_ref, o_ref, *, alpha):
    half = a_ref.shape[1]
    m = jnp.dot(a_ref[...].astype(BF16), w_ref[:half], preferred_element_type=F32)
    m = m + jnp.dot(b_ref[...].astype(BF16), w_ref[half:], preferred_element_type=F32)
    o_ref[...] = _layer_norm(alpha * x_ref[...] + m, g_ref[...], bb_ref[...])


def _oproj(x, lru_out, rwkv_out, w_o, g, b, *, alpha, tm):
    M, D = x.shape
    Wh = lru_out.shape[1]
    full = lambda a: pl.BlockSpec(a.shape, lambda i: (0,) * a.ndim)
    return pl.pallas_call(
        functools.partial(_oproj_kernel, alpha=alpha),
        grid=(M // tm,),
        in_specs=[pl.BlockSpec((tm, D), lambda i: (i, 0)),
                  pl.BlockSpec((tm, Wh), lambda i: (i, 0)),
                  pl.BlockSpec((tm, Wh), lambda i: (i, 0)),
                  full(w_o), full(g), full(b)],
        out_specs=pl.BlockSpec((tm, D), lambda i: (i, 0)),
        out_shape=jax.ShapeDtypeStruct((M, D), F32),
        compiler_params=_cparams("parallel"),
        name="out_proj_ln",
    )(x, lru_out, rwkv_out, w_o, g, b)


def _ffn_kernel(x_ref, wg_ref, wu_ref, wd_ref, g_ref, b_ref, o_ref, xb, acc, *, alpha):
    f = pl.program_id(1)

    @pl.when(f == 0)
    def _():
        xb[...] = x_ref[...].astype(BF16)
        acc[...] = jnp.zeros_like(acc)

    x = xb[...]
    hg = jnp.dot(x, wg_ref[...], preferred_element_type=F32)
    hu = jnp.dot(x, wu_ref[...], preferred_element_type=F32)
    h = (hg * _sigmoid(hg)) * hu
    acc[...] += jnp.dot(h.astype(BF16), wd_ref[...], preferred_element_type=F32)

    @pl.when(f == pl.num_programs(1) - 1)
    def _():
        o_ref[...] = _layer_norm(alpha * x_ref[...] + acc[...], g_ref[...], b_ref[...])


def _ffn(x, wg, wu, wd, g, b, *, alpha, tm, tf):
    M, D = x.shape
    F = wg.shape[1]
    full = lambda a: pl.BlockSpec(a.shape, lambda i, f: (0,) * a.ndim)
    return pl.pallas_call(
        functools.partial(_ffn_kernel, alpha=alpha),
        grid=(M // tm, F // tf),
        in_specs=[pl.BlockSpec((tm, D), lambda i, f: (i, 0)),
                  pl.BlockSpec((D, tf), lambda i, f: (0, f)),
                  pl.BlockSpec((D, tf), lambda i, f: (0, f)),
                  pl.BlockSpec((tf, D), lambda i, f: (f, 0)),
                  full(g), full(b)],
        out_specs=pl.BlockSpec((tm, D), lambda i, f: (i, 0)),
        out_shape=jax.ShapeDtypeStruct((M, D), F32),
        scratch_shapes=[pltpu.VMEM((tm, D), BF16), pltpu.VMEM((tm, D), F32)],
        compiler_params=_cparams("parallel", "arbitrary"),
        name="ffn_ln",
    )(x, wg, wu, wd, g, b)


def _router_kernel(x_ref, w_ref, o_ref, *, n_experts):
    logits = jnp.dot(x_ref[...], w_ref[...], precision=HI, preferred_element_type=F32)
    col = lax.broadcasted_iota(jnp.int32, logits.shape, 1)
    neg = jnp.float32(-jnp.inf)
    logits = jnp.where(col < n_experts, logits, neg)
    big = jnp.int32(logits.shape[1])
    m1 = jnp.max(logits, axis=-1, keepdims=True)
    i1 = jnp.min(jnp.where(logits == m1, col, big), axis=-1, keepdims=True)
    rest = jnp.where(col == i1, neg, logits)
    m2 = jnp.max(rest, axis=-1, keepdims=True)
    i2 = jnp.min(jnp.where(rest == m2, col, big), axis=-1, keepdims=True)
    e2 = jnp.exp(m2 - m1)
    p1 = 1.0 / (1.0 + e2)
    p2 = e2 / (1.0 + e2)
    o_ref[...] = (jnp.where(col == 0, p1, 0.0) + jnp.where(col == 1, p2, 0.0)
                  + jnp.where(col == 2, i1.astype(F32), 0.0)
                  + jnp.where(col == 3, i2.astype(F32), 0.0))


def _router(x, w_router_p, *, n_experts, tm):
    M, D = x.shape
    NP = w_router_p.shape[1]
    return pl.pallas_call(
        functools.partial(_router_kernel, n_experts=n_experts),
        grid=(M // tm,),
        in_specs=[pl.BlockSpec((tm, D), lambda i: (i, 0)),
                  pl.BlockSpec((D, NP), lambda i: (0, 0))],
        out_specs=pl.BlockSpec((tm, NP), lambda i: (i, 0)),
        out_shape=jax.ShapeDtypeStruct((M, NP), F32),
        compiler_params=_cparams("parallel"),
        name="router_top2",
    )(x, w_router_p)


def _experts_kernel(te_ref, tv_ref, src_ref, dst_ref, x_hbm, wg_ref, wu_ref, wd_ref,
                    out_hbm, rows, xb, acc, gsem, ssem, *, n_ff):
    i = pl.program_id(0)
    f = pl.program_id(1)
    TM = rows.shape[0]
    tf = wg_ref.shape[1]
    n_valid = tv_ref[i]
    active = n_valid > 0
    base = i * TM

    def gather_row(r, tok):
        return pltpu.make_async_copy(x_hbm.at[pl.ds(tok, 1)], rows.at[pl.ds(r, 1)], gsem)

    def scatter_row(r, dst):
        return pltpu.make_async_copy(rows.at[pl.ds(r, 1)], out_hbm.at[pl.ds(dst, 1)], ssem)

    def for_rows(fn):
        def body(r, _):
            fn(r)
            return 0
        lax.fori_loop(0, TM, body, 0, unroll=8)

    @pl.when(jnp.logical_and(active, f == 0))
    def _():
        for_rows(lambda r: gather_row(r, src_ref[base + r]).start())
        for_rows(lambda r: gather_row(r, 0).wait())
        xb[...] = rows[...].astype(BF16)
        acc[...] = jnp.zeros_like(acc)

    @pl.when(active)
    def _():
        valid = n_ff - f * tf
        col = lax.broadcasted_iota(jnp.int32, (1, tf), 1)
        row = lax.broadcasted_iota(jnp.int32, (tf, 1), 0)
        x = xb[...]
        hg = jnp.dot(x, wg_ref[...].astype(BF16), preferred_element_type=F32)
        hu = jnp.dot(x, wu_ref[...].astype(BF16), preferred_element_type=F32)
        h = jnp.where(col < valid, (hg * _sigmoid(hg)) * hu, 0.0)
        wd = jnp.where(row < valid, wd_ref[...], 0.0).astype(BF16)
        acc[...] += jnp.dot(h.astype(BF16), wd, preferred_element_type=F32)

    @pl.when(jnp.logical_and(active, f == pl.num_programs(1) - 1))
    def _():
        rows[...] = acc[...]

        def start(r):
            @pl.when(r < n_valid)
            def _():
                scatter_row(r, dst_ref[base + r]).start()

        def wait(r):
            @pl.when(r < n_valid)
            def _():
                scatter_row(r, 0).wait()

        for_rows(start)
        for_rows(wait)


def _experts(x, slots, exp_wg, exp_wu, exp_wd, *, layer, tm, tf):
    M, D = x.shape
    _, E, _, F = exp_wg.shape
    n_slots = 2 * M
    max_tiles = n_slots // tm + E
    n_f = pl.cdiv(F, tf)

    es = slots[:, 2:4].astype(jnp.int32).reshape(n_slots)
    oh = (es[:, None] == jnp.arange(E, dtype=jnp.int32)[None, :]).astype(jnp.int32)
    csum = jnp.cumsum(oh, axis=0)
    rank = jnp.sum((csum - oh) * oh, axis=1)
    counts = csum[-1]
    tiles = (counts + tm - 1) // tm
    tile_end = jnp.cumsum(tiles)
    tile_start = tile_end - tiles
    pos = tile_start[es] * tm + rank
    sid = jnp.arange(n_slots, dtype=jnp.int32)
    tok = sid // 2
    src = jnp.zeros((max_tiles * tm,), jnp.int32).at[pos].set(tok)
    dst = jnp.zeros((max_tiles * tm,), jnp.int32).at[pos].set((sid % 2) * M + tok)
    tid = jnp.arange(max_tiles, dtype=jnp.int32)
    te = jnp.searchsorted(tile_end, jnp.minimum(tid, tile_end[-1] - 1), side="right")
    te = te.astype(jnp.int32)
    tv = jnp.clip(counts[te] - (tid - tile_start[te]) * tm, 0, tm)
    tv = jnp.where(tid < tile_end[-1], tv, 0).astype(jnp.int32)

    f_idx = lambda i, f, tv: jnp.where(tv[i] > 0, f, n_f - 1)
    up_map = lambda i, f, te, tv, s, d: (layer, te[i], 0, f_idx(i, f, tv))
    down_map = lambda i, f, te, tv, s, d: (layer, te[i], f_idx(i, f, tv), 0)
    grid_spec = pltpu.PrefetchScalarGridSpec(
        num_scalar_prefetch=4,
        grid=(max_tiles, n_f),
        in_specs=[pl.BlockSpec(memory_space=pl.ANY),
                  pl.BlockSpec((None, None, D, tf), up_map),
                  pl.BlockSpec((None, None, D, tf), up_map),
                  pl.BlockSpec((None, None, tf, D), down_map)],
        out_specs=pl.BlockSpec(memory_space=pl.ANY),
        scratch_shapes=[pltpu.VMEM((tm, D), F32), pltpu.VMEM((tm, D), BF16),
                        pltpu.VMEM((tm, D), F32),
                        pltpu.SemaphoreType.DMA(()), pltpu.SemaphoreType.DMA(())],
    )
    return pl.pallas_call(
        functools.partial(_experts_kernel, n_ff=F),
        grid_spec=grid_spec,
        out_shape=jax.ShapeDtypeStruct((n_slots, D), F32),
        compiler_params=pltpu.CompilerParams(dimension_semantics=("arbitrary", "arbitrary"),
                                             vmem_limit_bytes=EXPERT_VMEM_LIMIT),
        name="moe_experts",
    )(te, tv, src, dst, x, exp_wg, exp_wu, exp_wd)


def _combine_kernel(x_ref, s_ref, o0_ref, o1_ref, g_ref, b_ref, o_ref, *, alpha):
    s = s_ref[...]
    f = s[:, 0:1] * o0_ref[...] + s[:, 1:2] * o1_ref[...]
    o_ref[...] = _layer_norm(alpha * x_ref[...] + f, g_ref[...], b_ref[...])


def _combine(x, slots, out2, g, b, *, alpha, tm):
    M, D = x.shape
    NP = slots.shape[1]
    nb = M // tm
    full = lambda a: pl.BlockSpec(a.shape, lambda i: (0,) * a.ndim)
    return pl.pallas_call(
        functools.partial(_combine_kernel, alpha=alpha),
        grid=(nb,),
        in_specs=[pl.BlockSpec((tm, D), lambda i: (i, 0)),
                  pl.BlockSpec((tm, NP), lambda i: (i, 0)),
                  pl.BlockSpec((tm, D), lambda i: (i, 0)),
                  pl.BlockSpec((tm, D), lambda i: (i + nb, 0)),
                  full(g), full(b)],
        out_specs=pl.BlockSpec((tm, D), lambda i: (i, 0)),
        out_shape=jax.ShapeDtypeStruct((M, D), F32),
        compiler_params=_cparams("parallel"),
        name="moe_combine_ln",
    )(x, slots, out2, out2, g, b)


def _pad_to(a, axis, mult):
    n = a.shape[axis]
    pad = (-n) % mult
    if pad == 0:
        return a
    widths = [(0, 0)] * a.ndim
    widths[axis] = (0, pad)
    return jnp.pad(a, widths)


def _block_diag(w, group):
    n, d, e = w.shape
    wg = w.reshape(n // group, group, d, e)
    eye = jnp.eye(group, dtype=w.dtype)
    return jnp.einsum('gqde,qr->gqdre', wg, eye).reshape(n // group, group * d, group * e)


def kernel(x_prompt, x_sample, state_lru_conv, state_lru_h, state_rwkv_shift, state_rwkv_wkv,
           w_in, conv_w, conv_b, lru_wa, lru_ba, lru_wx, lru_bx, lru_lambda,
           mu_shift, w0, w2, a0, a2, g2, k_k, k_a, r_k, gn_g, gn_b, w_o,
           ln1_g, ln1_b, ln2_g, ln2_b, ffn_wg, ffn_wu, ffn_wd,
           w_router, exp_wg, exp_wu, exp_wd):
    Bp, Tp, D = x_prompt.shape
    Bs, Ts, _ = x_sample.shape
    depth = w_in.shape[0]
    W = conv_w.shape[2]
    n_rkv = 3 * W
    n_main = 2 * W + n_rkv
    n_lr = w_in.shape[2] - n_main
    r_decay, r_aaa = w2.shape[1], a2.shape[1]
    n_lr_p = n_lr + (-n_lr) % LANES
    alpha = (2.0 * depth) ** 0.25
    lead = SAMPLE_TP - Ts
    Mp, Ms = Bp * Tp, Bs * Ts
    group = 2 * LANES // HEAD_DIM

    row = lambda a: a.reshape(1, -1)
    head_ones = jnp.kron(jnp.eye(W // HEAD_DIM, dtype=F32),
                         jnp.ones((HEAD_DIM, HEAD_DIM), F32)).astype(BF16)
    x = jnp.concatenate([x_prompt.reshape(Mp, D), x_sample.reshape(Ms, D)], axis=0)
    outs = {k: [] for k in ("pc", "ph", "ps", "pw", "sc", "sh", "ss", "sw")}

    for l in range(depth):
        w_main = w_in[l, :, :n_main].astype(BF16)
        w_lr = _pad_to(w_in[l, :, n_main:], 1, LANES).astype(BF16)
        proj = _matmul(x, w_main, TOKEN_TILE, min(512, W))
        proj_lr = _matmul(x, w_lr, TOKEN_TILE, n_lr_p)

        st_main = jnp.zeros((Bs, lead, n_main), F32)
        st_main = st_main.at[:, lead - (CONV_W - 1):, :W].set(state_lru_conv[l])
        st_main = st_main.at[:, lead - 1, 2 * W:].set(state_rwkv_shift[l, :, :n_rkv])
        proj_s = jnp.concatenate([st_main, proj[Mp:].reshape(Bs, Ts, n_main)], axis=1)
        proj_s = proj_s.reshape(Bs * SAMPLE_TP, n_main)
        st_lr = jnp.zeros((Bs, lead, n_lr_p), F32)
        st_lr = st_lr.at[:, lead - 1, :n_lr].set(state_rwkv_shift[l, :, n_rkv:])
        lr_s = jnp.concatenate([st_lr, proj_lr[Mp:].reshape(Bs, Ts, n_lr_p)], axis=1)
        lr_s = lr_s.reshape(Bs * SAMPLE_TP, n_lr_p)
        h0_ext = jnp.zeros((Bs, SAMPLE_TP, W), F32).at[:, lead - 1].set(state_lru_h[l])
        h0_ext = h0_ext.reshape(Bs * SAMPLE_TP, W)
        proj_p = proj[:Mp]
        lr_p = proj_lr[:Mp]

        wbd = jnp.concatenate([_block_diag(lru_wa[l], group), _block_diag(lru_wx[l], group)],
                              axis=2).astype(BF16)
        bias = jnp.stack([lru_ba[l], lru_bx[l]])
        lru_args = (conv_w[l], row(conv_b[l]), wbd, bias, row(lru_lambda[l]))
        lru_p, h_p = _lru(proj, None, *lru_args, rows=Mp, blocks_per_seq=Tp // TIME_ROWS,
                          tp=Tp, lead=0)
        lru_s, h_s = _lru(proj_s, h0_ext, *lru_args, rows=Bs * SAMPLE_TP, blocks_per_seq=1,
                          tp=SAMPLE_TP, lead=lead)

        mu = mu_shift[l, :n_rkv].reshape(3, W)
        mulr = row(_pad_to(mu_shift[l, n_rkv:], 0, LANES))
        zeros_lr = lambda n: jnp.zeros((n, W), F32)
        w2p = jnp.concatenate([w2[l], zeros_lr(n_lr_p - r_decay)], axis=0).astype(BF16)
        a2p = jnp.concatenate([zeros_lr(r_decay), a2[l], zeros_lr(n_lr_p - r_decay - r_aaa)],
                              axis=0).astype(BF16)
        g2p = jnp.concatenate([zeros_lr(r_decay + r_aaa), g2[l], zeros_lr(n_lr_p - n_lr)],
                              axis=0).astype(BF16)
        prep_params = (mu, mulr, row(w0[l]), row(a0[l]), row(k_k[l]), row(k_a[l]),
                       w2p, a2p, g2p, head_ones)
        ops_p = _prep(proj, proj_lr, prep_params, rows=Mp, col0=2,
                      blocks_per_seq=Tp // TIME_ROWS, tp=Tp, lead=0, chunk=WKV_CHUNK)
        ops_s = _prep(proj_s, lr_s, prep_params, rows=Bs * SAMPLE_TP, col0=2, blocks_per_seq=1,
                      tp=SAMPLE_TP, lead=lead, chunk=SAMPLE_TP)
        head_args = (row(r_k[l]), row(gn_g[l]), row(gn_b[l]))
        rwkv_p, S_p = _wkv_prompt(ops_p, *head_args, n_seq=Bp, tp=Tp, rows_blk=WKV_ROWS,
                                  chunk=WKV_CHUNK, width=min(W, WKV_WIDTH))
        rwkv_s, S_s = _wkv_sample(ops_s, state_rwkv_wkv[l], *head_args, n_seq=Bs, tp=SAMPLE_TP,
                                  rows_blk=WKV_SAMPLE_ROWS, width=min(W, WKV_SAMPLE_WIDTH),
                                  group=WKV_SAMPLE_GROUP)

        real = lambda a: a.reshape(Bs, SAMPLE_TP, -1)[:, lead:].reshape(Ms, -1)
        lru_out = jnp.concatenate([lru_p, real(lru_s)], axis=0)
        rwkv_out = jnp.concatenate([rwkv_p, real(rwkv_s)], axis=0)
        x = _oproj(x, lru_out, rwkv_out, w_o[l].astype(BF16), row(ln1_g[l]), row(ln1_b[l]),
                   alpha=alpha, tm=256)

        pm = proj_p.reshape(Bp, Tp, n_main)
        outs["pc"].append(pm[:, Tp - (CONV_W - 1):, :W])
        outs["ph"].append(h_p.reshape(Bp, Tp, W)[:, -1])
        outs["ps"].append(jnp.concatenate([pm[:, -1, 2 * W:],
                                           lr_p.reshape(Bp, Tp, n_lr_p)[:, -1, :n_lr]], axis=-1))
        outs["pw"].append(S_p)
        sm = proj_s.reshape(Bs, SAMPLE_TP, n_main)
        outs["sc"].append(sm[:, SAMPLE_TP - (CONV_W - 1):, :W])
        outs["sh"].append(h_s.reshape(Bs, SAMPLE_TP, W)[:, -1])
        outs["ss"].append(jnp.concatenate([sm[:, -1, 2 * W:],
                                           lr_s.reshape(Bs, SAMPLE_TP, n_lr_p)[:, -1, :n_lr]],
                                          axis=-1))
        outs["sw"].append(S_s)

        j = l // 2
        if l % 2 == 0:
            wg = _pad_to(ffn_wg[j], 1, FF_TILE).astype(BF16)
            wu = _pad_to(ffn_wu[j], 1, FF_TILE).astype(BF16)
            wd = _pad_to(ffn_wd[j], 0, FF_TILE).astype(BF16)
            x = _ffn(x, wg, wu, wd, row(ln2_g[l]), row(ln2_b[l]),
                     alpha=alpha, tm=TOKEN_TILE, tf=FF_TILE)
        else:
            slots = _router(x, _pad_to(w_router[j], 1, LANES), n_experts=exp_wg.shape[1],
                            tm=TOKEN_TILE)
            out2 = _experts(x, slots, exp_wg, exp_wu, exp_wd, layer=j, tm=EXPERT_TILE, tf=FF_TILE)
            x = _combine(x, slots, out2, row(ln2_g[l]), row(ln2_b[l]), alpha=alpha, tm=TOKEN_TILE)

    y_prompt = x[:Mp].reshape(Bp, Tp, D)
    y_sample = x[Mp:].reshape(Bs, Ts, D)
    st = lambda k: jnp.stack(outs[k])
    return (y_prompt, y_sample, st("pc"), st("ph"), st("ps"), st("pw"),
            st("sc"), st("sh"), st("ss"), st("sw"))
```

```python
import functools

import jax
import jax.numpy as jnp
from jax import lax
from jax.experimental import pallas as pl
from jax.experimental.pallas import tpu as pltpu

F32 = jnp.float32
BF16 = jnp.bfloat16
HI = lax.Precision.HIGHEST

HEAD_DIM = 64
CONV_W = 4
LRU_C = 8.0
N_EXPERTS = 8
LN_EPS = 1e-5
GN_EPS = 64e-5

LANES = 128
SUBLANES = 8
VMEM_LIMIT = 48 * 1024 * 1024
SAMPLE_TP = 8
TIME_ROWS = 256
TOKEN_TILE = 512
PROJ_ROWS = 1088
FF_TILE = 512
EXPERT_SUB = 768
EXPERT_SUBS = 3
EXPERT_FF_TILE = 256
EXPERT_VMEM_LIMIT = 56 * 1024 * 1024
WKV_CHUNK = 64
WKV_ROWS = 256
WKV_WIDTH = 1024
PRE_CHUNKS = 2
WKV_SAMPLE_ROWS = 128
WKV_SAMPLE_WIDTH = 256
WKV_SAMPLE_GROUP = 8


def _cparams(*sem):
    return pltpu.CompilerParams(dimension_semantics=sem, vmem_limit_bytes=VMEM_LIMIT)


def _sigmoid(x):
    return 1.0 / (1.0 + jnp.exp(-x))


def _softplus(x):
    return jnp.maximum(x, 0.0) + jnp.log1p(jnp.exp(-jnp.abs(x)))


def _gelu_tanh(x):
    c = 0.7978845608028654
    return 0.5 * x * (1.0 + jnp.tanh(c * (x + 0.044715 * (x * x * x))))


def _layer_norm(x, g, b):
    mu = jnp.mean(x, axis=-1, keepdims=True)
    xc = x - mu
    var = jnp.mean(xc * xc, axis=-1, keepdims=True)
    return xc * lax.rsqrt(var + LN_EPS) * g + b


def _shift_rows(x, carry, k):
    xr = pltpu.roll(x, k, 0)
    cr = pltpu.roll(carry, k, 0)
    row = lax.broadcasted_iota(jnp.int32, carry.shape, 0)
    first = jnp.where(row < k, cr, xr[:SUBLANES])
    return jnp.concatenate([first, xr[SUBLANES:]], axis=0)


def _row_tile(m, target):
    return max(t for t in range(SUBLANES, target + 1, SUBLANES) if m % t == 0)


def _mm_kernel(x_ref, w_ref, o_ref, xb):
    @pl.when(pl.program_id(1) == 0)
    def _():
        xb[...] = x_ref[...].astype(BF16)

    o_ref[...] = jnp.dot(xb[...], w_ref[...], preferred_element_type=F32)


def _matmul(x, w, tm, tn):
    M, K = x.shape
    N = w.shape[1]
    return pl.pallas_call(
        _mm_kernel,
        grid=(M // tm, N // tn),
        in_specs=[pl.BlockSpec((tm, K), lambda i, j: (i, 0)),
                  pl.BlockSpec((K, tn), lambda i, j: (0, j))],
        out_specs=pl.BlockSpec((tm, tn), lambda i, j: (i, j)),
        out_shape=jax.ShapeDtypeStruct((M, N), F32),
        scratch_shapes=[pltpu.VMEM((tm, K), BF16)],
        compiler_params=_cparams("parallel", "arbitrary"),
        name="in_proj",
    )(x, w)


def _lru_kernel(*refs, blocks_per_seq, tp, lead):
    if lead:
        (xb_ref, gate_ref, h0_ref, cw_ref, cb_ref, wbd_ref, bias_ref, lam_ref,
         out_ref, h_ref, xcarry, hcarry, a_s, u_s) = refs
    else:
        (xb_ref, gate_ref, cw_ref, cb_ref, wbd_ref, bias_ref, lam_ref,
         out_ref, xtail_ref, htail_ref, xcarry, hcarry, a_s, u_s, h_ref) = refs
    R, W = xb_ref.shape
    i = pl.program_id(0)

    @pl.when(i % blocks_per_seq == 0)
    def _():
        xcarry[...] = jnp.zeros_like(xcarry)
        hcarry[...] = jnp.zeros_like(hcarry)

    xb = xb_ref[...]
    carry = xcarry[...]
    cw = cw_ref[...]
    xc = cb_ref[...] + cw[3:4] * xb
    for k in range(1, CONV_W):
        xc = xc + cw[3 - k:4 - k] * _shift_rows(xb, carry, k)
    xcarry[...] = xb[R - SUBLANES:]

    xcb = xc.astype(BF16)
    nb = wbd_ref.shape[0]
    bw = W // nb
    pre_a, pre_x = [], []
    for g in range(nb):
        pre = jnp.dot(xcb[:, g * bw:(g + 1) * bw], wbd_ref[g], preferred_element_type=F32)
        pre_a.append(pre[:, :bw])
        pre_x.append(pre[:, bw:])
    bias = bias_ref[...]
    r = _sigmoid(jnp.concatenate(pre_a, axis=1) + bias[0:1])
    ig = _sigmoid(jnp.concatenate(pre_x, axis=1) + bias[1:2])
    log_a = -LRU_C * r * _softplus(-lam_ref[...])
    a = jnp.exp(log_a)
    u = jnp.sqrt(-jnp.tanh(log_a) * (a * a + 1.0)) * (ig * xc)
    if lead:
        row = lax.broadcasted_iota(jnp.int32, (R, W), 0)
        is_state = (row % tp) == (lead - 1)
        a = jnp.where(is_state, 0.0, a)
        u = jnp.where(is_state, h0_ref[...], u)
    a_s[...] = a
    u_s[...] = u

    def tile_body(t, h):
        off = pl.multiple_of(t * SUBLANES, SUBLANES)
        a8 = a_s[pl.ds(off, SUBLANES), :]
        u8 = u_s[pl.ds(off, SUBLANES), :]
        rows = []
        for j in range(SUBLANES):
            h = a8[j:j + 1] * h + u8[j:j + 1]
            rows.append(h)
        h_ref[pl.ds(off, SUBLANES), :] = jnp.concatenate(rows, axis=0)
        return h

    h_last = lax.fori_loop(0, R // SUBLANES, tile_body, hcarry[0:1])
    hcarry[...] = jnp.broadcast_to(h_last, hcarry.shape)
    out_ref[...] = h_ref[...] * _gelu_tanh(gate_ref[...])
    if not lead:
        xtail_ref[...] = xb[R - SUBLANES:]
        htail_ref[...] = hcarry[...]


def _lru(proj, h0_ext, cw, cb, wbd, bias, lam, *, rows, blocks_per_seq, tp, lead):
    W = cw.shape[1]
    R = TIME_ROWS
    row_blk = lambda c: pl.BlockSpec((R, W), lambda i, c=c: (i, c))
    full = lambda a: pl.BlockSpec(a.shape, lambda i: (0,) * a.ndim)
    ins = [proj, proj]
    specs = [row_blk(0), row_blk(1)]
    scratch = [pltpu.VMEM((SUBLANES, W), F32), pltpu.VMEM((SUBLANES, W), F32),
               pltpu.VMEM((R, W), F32), pltpu.VMEM((R, W), F32)]
    if lead:
        ins.append(h0_ext)
        specs.append(row_blk(0))
        out_specs = [row_blk(0), row_blk(0)]
        out_shape = [jax.ShapeDtypeStruct((rows, W), F32)] * 2
    else:
        n_seq = rows // (R * blocks_per_seq)
        tail = pl.BlockSpec((None, SUBLANES, W), lambda i: (i // blocks_per_seq, 0, 0))
        out_specs = [row_blk(0), tail, tail]
        out_shape = ([jax.ShapeDtypeStruct((rows, W), F32)]
                     + [jax.ShapeDtypeStruct((n_seq, SUBLANES, W), F32)] * 2)
        scratch.append(pltpu.VMEM((R, W), F32))
    params = [cw, cb, wbd, bias, lam]
    ins += params
    specs += [full(p) for p in params]
    return pl.pallas_call(
        functools.partial(_lru_kernel, blocks_per_seq=blocks_per_seq, tp=tp, lead=lead),
        grid=(rows // R,),
        in_specs=specs,
        out_specs=out_specs,
        out_shape=out_shape,
        scratch_shapes=scratch,
        compiler_params=_cparams("arbitrary"),
        name="rg_lru",
    )(*ins)


def _prep_kernel(r_ref, k_ref, v_ref, lr_ref, mu_ref, mulr_ref, w0_ref, a0_ref, kkw_ref, kaw_ref,
                 w2_ref, a2_ref, g2_ref, ones_ref,
                 ro_ref, km_ref, vo_ref, lw_ref, cum_ref, kk_ref, bk_ref, g_ref,
                 rt_ref, kt_ref, vt_ref, lt_ref,
                 cr, ck, cv, clr, *, blocks_per_seq, tp, lead, chunk):
    i = pl.program_id(0)

    @pl.when(i % blocks_per_seq == 0)
    def _():
        for c in (cr, ck, cv, clr):
            c[...] = jnp.zeros_like(c)

    def token_shift(x_ref, c_ref, mu):
        x = x_ref[...]
        prev = _shift_rows(x, c_ref[...], 1)
        c_ref[...] = x[x.shape[0] - SUBLANES:]
        return x + (prev - x) * mu

    mu = mu_ref[...]
    r = token_shift(r_ref, cr, mu[0:1])
    k = token_shift(k_ref, ck, mu[1:2])
    v = token_shift(v_ref, cv, mu[2:3])
    xl = token_shift(lr_ref, clr, mulr_ref[...])
    dw = jnp.dot(jnp.tanh(xl).astype(BF16), w2_ref[...], preferred_element_type=F32)
    w_log = -_softplus(-(w0_ref[...] + dw)) - 0.5
    lw = -jnp.exp(w_log)
    da = jnp.dot(xl.astype(BF16), a2_ref[...], preferred_element_type=F32)
    ah = _sigmoid(a0_ref[...] + da)
    g_ref[...] = jnp.dot(_sigmoid(xl).astype(BF16), g2_ref[...], preferred_element_type=F32)

    R = r.shape[0]
    if lead:
        t = lax.broadcasted_iota(jnp.int32, r.shape, 0) % tp
        real = t >= lead
        k = jnp.where(real, k, 0.0)
        v = jnp.where(real, v, 0.0)
        lw = jnp.where(real, lw, 0.0)

    kk = k * kkw_ref[...]
    sq = kk * kk
    sq_hi = sq.astype(BF16)
    sq_lo = (sq - sq_hi.astype(F32)).astype(BF16)
    ss = (jnp.dot(sq_hi, ones_ref[...], preferred_element_type=F32)
          + jnp.dot(sq_lo, ones_ref[...], preferred_element_type=F32))
    kk = kk / jnp.maximum(jnp.sqrt(ss), 1e-12)

    shift = chunk.bit_length() - 1
    rr = lax.broadcasted_iota(jnp.int32, (R, R), 0)
    cc = lax.broadcasted_iota(jnp.int32, (R, R), 1)
    same = jnp.right_shift(rr, shift) == jnp.right_shift(cc, shift)
    tri = jnp.where(jnp.logical_and(same, cc <= rr), 1.0, 0.0)
    cum_ref[...] = jnp.dot(tri, lw, precision=HI, preferred_element_type=F32)

    ro_ref[...] = r
    km_ref[...] = k * (1.0 + (ah - 1.0) * kaw_ref[...])
    vo_ref[...] = v
    lw_ref[...] = lw
    kk_ref[...] = kk
    bk_ref[...] = kk * ah
    for t_ref, c_ref in ((rt_ref, cr), (kt_ref, ck), (vt_ref, cv), (lt_ref, clr)):
        t_ref[...] = c_ref[...]


def _prep(proj, proj_lr, params, *, rows, col0, blocks_per_seq, tp, lead, chunk):
    W = params[2].shape[1]
    LR = proj_lr.shape[1]
    R = TIME_ROWS
    n_grp = rows // (R * blocks_per_seq)
    row_blk = lambda c: pl.BlockSpec((R, W), lambda i, c=c: (i, c))
    full = lambda a: pl.BlockSpec(a.shape, lambda i: (0,) * a.ndim)
    tail = lambda n: pl.BlockSpec((None, SUBLANES, n), lambda i: (i // blocks_per_seq, 0, 0))
    outs = pl.pallas_call(
        functools.partial(_prep_kernel, blocks_per_seq=blocks_per_seq, tp=tp, lead=lead,
                          chunk=chunk),
        grid=(rows // R,),
        in_specs=[row_blk(col0), row_blk(col0 + 1), row_blk(col0 + 2),
                  pl.BlockSpec((R, LR), lambda i: (i, 0))] + [full(p) for p in params],
        out_specs=[row_blk(0)] * 8 + [tail(W)] * 3 + [tail(LR)],
        out_shape=([jax.ShapeDtypeStruct((rows, W), F32)] * 8
                   + [jax.ShapeDtypeStruct((n_grp, SUBLANES, W), F32)] * 3
                   + [jax.ShapeDtypeStruct((n_grp, SUBLANES, LR), F32)]),
        scratch_shapes=[pltpu.VMEM((SUBLANES, W), F32)] * 3 + [pltpu.VMEM((SUBLANES, LR), F32)],
        compiler_params=_cparams("arbitrary"),
        name="rwkv_prep",
    )(proj, proj, proj, proj_lr, *params)
    return outs[:8], outs[8:]


def _mm(a, b):
    return jnp.dot(a.astype(BF16), b.astype(BF16), preferred_element_type=F32)


def _mm_nt(a, b):
    return lax.dot_general(a.astype(BF16), b.astype(BF16), (((1,), (1,)), ((), ())),
                           preferred_element_type=F32)


def _mm_tn(a, b):
    return lax.dot_general(a.astype(BF16), b.astype(BF16), (((0,), (0,)), ((), ())),
                           preferred_element_type=F32)


def _pairs_pre(tiles):
    C = tiles[0][0].shape[0]
    N = HEAD_DIM
    n = range(len(tiles))
    h0 = lax.broadcasted_iota(jnp.int32, (C, LANES), 1) < N
    stack = lambda x: jnp.concatenate([jnp.where(h0, x, 0.0), jnp.where(h0, 0.0, x)], axis=0)
    ri = lax.broadcasted_iota(jnp.int32, (2 * C, 2 * C), 0)
    ci = lax.broadcasted_iota(jnp.int32, (2 * C, 2 * C), 1)
    strict = (ci & (C - 1)) < (ri & (C - 1))
    incl = (ci & (C - 1)) <= (ri & (C - 1))
    eye = jnp.where(ri == ci, 1.0, 0.0)

    lhs, rhs, vs, zs, gl = [], [], [], [], []
    for r, kmod, v, lw, cum, kk, bk in tiles:
        g_incl = jnp.exp(cum)
        g_excl = jnp.exp(cum - lw)
        g_inv = jnp.exp(-cum)
        g_rest = jnp.exp(cum[C - 1:C] - cum)
        lhs.append(jnp.concatenate([stack(-(kk * g_excl)), stack(r * g_incl)],
                                   axis=0).astype(BF16))
        rhs.append(jnp.concatenate([stack(bk * g_inv), stack(kmod * g_inv)],
                                   axis=0).astype(BF16))
        vs.append(stack(v).astype(BF16))
        zs.append(jnp.concatenate([stack(bk * g_rest), stack(kmod * g_rest)],
                                  axis=0).astype(BF16))
        gl.append(g_incl[C - 1:C])

    aa = [_mm_nt(lhs[i], rhs[i]) for i in n]
    a_ab = [jnp.where(strict, aa[i][:2 * C, :2 * C], 0.0) for i in n]
    a_ak = [jnp.where(strict, aa[i][:2 * C, 2 * C:], 0.0).astype(BF16) for i in n]
    arbk = [jnp.concatenate([jnp.where(incl, aa[i][2 * C:, :2 * C], 0.0),
                             jnp.where(incl, aa[i][2 * C:, 2 * C:], 0.0)],
                            axis=1).astype(BF16) for i in n]
    av = [_mm(a_ak[i], vs[i]) for i in n]

    t = [eye + a_ab[i] for i in n]
    n_iter = max(1, (C - 1).bit_length())
    if n_iter > 1:
        p = [_mm(a_ab[i], a_ab[i]) for i in n]
    for it in range(1, n_iter):
        if it + 1 < n_iter:
            pt = [_mm(p[i], jnp.concatenate([t[i], p[i]], axis=1)) for i in n]
            t = [t[i] + pt[i][:, :2 * C] for i in n]
            p = [pt[i][:, 2 * C:] for i in n]
        else:
            t = [t[i] + _mm(p[i], t[i]) for i in n]
    t = [t[i].astype(BF16) for i in n]
    tav = [_mm(t[i], av[i]) for i in n]
    return [(lhs[i], t[i], tav[i], arbk[i], vs[i], zs[i], gl[i]) for i in n]


def _pairs_step(states, pres):
    n = range(len(states))
    C2 = pres[0][1].shape[0]
    ls = [_mm_nt(pres[i][0], states[i]) for i in n]
    u = [_mm(pres[i][1], ls[i][:C2]) + pres[i][2] for i in n]
    xs = [jnp.concatenate([u[i].astype(BF16), pres[i][4]], axis=0) for i in n]
    s1 = [states[i] * pres[i][6] + _mm_tn(xs[i], pres[i][5]) for i in n]
    ys = [ls[i][C2:] + _mm(pres[i][3], xs[i]) for i in n]
    return [(ys[i][:C2 // 2] + ys[i][C2 // 2:], s1[i]) for i in n]


def _wkv_epilogue(y, r, kmod, v, g, rk, gg, gb):
    N = HEAD_DIM
    h0 = lax.broadcasted_iota(jnp.int32, y.shape, 1) < N

    def head_sum(x):
        s0 = jnp.sum(jnp.where(h0, x, 0.0), axis=-1, keepdims=True)
        s1 = jnp.sum(jnp.where(h0, 0.0, x), axis=-1, keepdims=True)
        return jnp.where(h0, s0, s1)

    yc = y - head_sum(y) * (1.0 / N)
    var = head_sum(yc * yc) * (1.0 / N)
    yn = yc * lax.rsqrt(var + GN_EPS) * gg + gb
    bonus = head_sum(r * kmod * rk) * v
    return (yn + bonus) * g


def _wkv_prompt_kernel(r_ref, km_ref, v_ref, lw_ref, cum_ref, kk_ref, bk_ref, g_ref,
                       rk_ref, gg_ref, gb_ref, o_ref, so_ref,
                       st, lhs_s, t_s, tav_s, arbk_s, vs_s, zs_s, gl_s, *, chunk):
    N = HEAD_DIM
    C = chunk
    rows_blk, width = r_ref.shape
    pairs = width // LANES
    n_chunks = rows_blk // C
    tb = pl.program_id(2)

    @pl.when(tb == 0)
    def _():
        st[...] = jnp.zeros_like(st)

    in_refs = (r_ref, km_ref, v_ref, lw_ref, cum_ref, kk_ref, bk_ref)
    pre_refs = (lhs_s, t_s, tav_s, arbk_s, vs_s, zs_s)
    lanes = [slice(p * LANES, (p + 1) * LANES) for p in range(pairs)]

    def pre_body(c2, _):
        where = [(c2 * PRE_CHUNKS + j, p) for j in range(PRE_CHUNKS) for p in range(pairs)]
        tiles = []
        for c, p in where:
            rows = pl.ds(pl.multiple_of(c * C, C), C)
            tiles.append(tuple(ref[rows, lanes[p]] for ref in in_refs))
        for (c, p), pre in zip(where, _pairs_pre(tiles)):
            for ref, val in zip(pre_refs, pre[:6]):
                ref[c, p] = val
            gl_s[c, p] = jnp.broadcast_to(pre[6], (SUBLANES, LANES))
        return 0

    lax.fori_loop(0, n_chunks // PRE_CHUNKS, pre_body, 0)

    def step_body(c, _):
        rows = pl.ds(pl.multiple_of(c * C, C), C)
        pres = [tuple(ref[c, p] for ref in pre_refs) + (gl_s[c, p][0:1],) for p in range(pairs)]
        outs = _pairs_step([st[p] for p in range(pairs)], pres)
        for p, (y, s1) in enumerate(outs):
            st[p] = s1
            o_ref[rows, lanes[p]] = _wkv_epilogue(
                y, r_ref[rows, lanes[p]], km_ref[rows, lanes[p]], v_ref[rows, lanes[p]],
                g_ref[rows, lanes[p]], rk_ref[:, lanes[p]], gg_ref[:, lanes[p]],
                gb_ref[:, lanes[p]])
        return 0

    lax.fori_loop(0, n_chunks, step_body, 0)

    @pl.when(tb == pl.num_programs(2) - 1)
    def _():
        for p in range(pairs):
            sb = st[p]
            so_ref[0, 2 * p] = sb[:N, :N]
            so_ref[0, 2 * p + 1] = sb[N:, N:]


def _wkv_sample_kernel(r_ref, km_ref, v_ref, lw_ref, cum_ref, kk_ref, bk_ref, g_ref, s0_ref,
                       rk_ref, gg_ref, gb_ref, o_ref, so_ref, *, tp, group):
    N = HEAD_DIM
    rows_blk, width = r_ref.shape
    pairs = width // LANES
    n_seq = rows_blk // tp
    zero = jnp.zeros((N, N), F32)

    in_refs = (r_ref, km_ref, v_ref, lw_ref, cum_ref, kk_ref, bk_ref)
    lanes = [slice(p * LANES, (p + 1) * LANES) for p in range(pairs)]

    def group_body(gi, _):
        where = [(gi * group + j, p) for j in range(group) for p in range(pairs)]
        tiles, states = [], []
        for s, p in where:
            rows = pl.ds(pl.multiple_of(s * tp, SUBLANES), tp)
            tiles.append(tuple(ref[rows, lanes[p]] for ref in in_refs))
            top = jnp.concatenate([s0_ref[s, 2 * p], zero], axis=1)
            bot = jnp.concatenate([zero, s0_ref[s, 2 * p + 1]], axis=1)
            states.append(jnp.concatenate([top, bot], axis=0))
        outs = _pairs_step(states, _pairs_pre(tiles))
        for (s, p), tile, (y, s1) in zip(where, tiles, outs):
            rows = pl.ds(pl.multiple_of(s * tp, SUBLANES), tp)
            so_ref[s, 2 * p] = s1[:N, :N]
            so_ref[s, 2 * p + 1] = s1[N:, N:]
            o_ref[rows, lanes[p]] = _wkv_epilogue(
                y, tile[0], tile[1], tile[2], g_ref[rows, lanes[p]],
                rk_ref[:, lanes[p]], gg_ref[:, lanes[p]], gb_ref[:, lanes[p]])
        return 0

    lax.fori_loop(0, n_seq // group, group_body, 0)


def _wkv_prompt(ops, rk_w, gg_w, gb_w, *, n_seq, tp, rows_blk, chunk, width):
    rows, W = ops[0].shape
    N = HEAD_DIM
    H = W // N
    hq = width // N
    pairs = width // LANES
    tbs = tp // rows_blk
    n_chunks = rows_blk // chunk
    C2, C4 = 2 * chunk, 4 * chunk
    blk = pl.BlockSpec((rows_blk, width), lambda i, q, t: (i * tbs + t, q))
    pblk = pl.BlockSpec((1, width), lambda i, q, t: (0, q))
    sblk = pl.BlockSpec((1, hq, N, N), lambda i, q, t: (i, q, 0, 0))
    per_chunk = lambda shape, dt: pltpu.VMEM((n_chunks, pairs) + shape, dt)
    return pl.pallas_call(
        functools.partial(_wkv_prompt_kernel, chunk=chunk),
        grid=(n_seq, H // hq, tbs),
        in_specs=[blk] * len(ops) + [pblk] * 3,
        out_specs=[blk, sblk],
        out_shape=[jax.ShapeDtypeStruct((rows, W), F32),
                   jax.ShapeDtypeStruct((n_seq, H, N, N), F32)],
        scratch_shapes=[pltpu.VMEM((pairs, LANES, LANES), F32),
                        per_chunk((C4, LANES), BF16), per_chunk((C2, C2), BF16),
                        per_chunk((C2, LANES), F32), per_chunk((C2, C4), BF16),
                        per_chunk((C2, LANES), BF16), per_chunk((C4, LANES), BF16),
                        per_chunk((SUBLANES, LANES), F32)],
        compiler_params=_cparams("parallel", "parallel", "arbitrary"),
        name="wkv7_prompt",
    )(*ops, rk_w, gg_w, gb_w)


def _wkv_sample(ops, s0, rk_w, gg_w, gb_w, *, n_seq, tp, rows_blk, width, group):
    rows, W = ops[0].shape
    N = HEAD_DIM
    H = W // N
    hq = width // N
    seq_blk = rows_blk // tp
    blk = pl.BlockSpec((rows_blk, width), lambda i, q: (i, q))
    pblk = pl.BlockSpec((1, width), lambda i, q: (0, q))
    sblk = pl.BlockSpec((seq_blk, hq, N, N), lambda i, q: (i, q, 0, 0))
    return pl.pallas_call(
        functools.partial(_wkv_sample_kernel, tp=tp, group=group),
        grid=(n_seq // seq_blk, H // hq),
        in_specs=[blk] * len(ops) + [sblk] + [pblk] * 3,
        out_specs=[blk, sblk],
        out_shape=[jax.ShapeDtypeStruct((rows, W), F32),
                   jax.ShapeDtypeStruct((n_seq, H, N, N), F32)],
        compiler_params=_cparams("parallel", "parallel"),
        name="wkv7_sample",
    )(*ops, s0, rk_w, gg_w, gb_w)


def _oproj_kernel(x_ref, a_ref, b_ref, w_ref, g_ref, bb_ref, o_ref, *, alpha):
    half = a_ref.shape[1]
    m = jnp.dot(a_ref[...].astype(BF16), w_ref[:half], preferred_element_type=F32)
    m = m + jnp.dot(b_ref[...].astype(BF16), w_ref[half:], preferred_element_type=F32)
    o_ref[...] = _layer_norm(alpha * x_ref[...] + m, g_ref[...], bb_ref[...])


def _oproj(x, lru_out, rwkv_out, w_o, g, b, *, alpha, tm):
    M, D = x.shape
    Wh = lru_out.shape[1]
    full = lambda a: pl.BlockSpec(a.shape, lambda i: (0,) * a.ndim)
    return pl.pallas_call(
        functools.partial(_oproj_kernel, alpha=alpha),
        grid=(M // tm,),
        in_specs=[pl.BlockSpec((tm, D), lambda i: (i, 0)),
                  pl.BlockSpec((tm, Wh), lambda i: (i, 0)),
                  pl.BlockSpec((tm, Wh), lambda i: (i, 0)),
                  full(w_o), full(g), full(b)],
        out_specs=pl.BlockSpec((tm, D), lambda i: (i, 0)),
        out_shape=jax.ShapeDtypeStruct((M, D), F32),
        compiler_params=_cparams("parallel"),
        name="out_proj_ln",
    )(x, lru_out, rwkv_out, w_o, g, b)


def _ffn_kernel(x_ref, wg_ref, wu_ref, wd_ref, g_ref, b_ref, o_ref, xb, acc, *, alpha):
    f = pl.program_id(1)

    @pl.when(f == 0)
    def _():
        xb[...] = x_ref[...].astype(BF16)
        acc[...] = jnp.zeros_like(acc)

    x = xb[...]
    hg = jnp.dot(x, wg_ref[...], preferred_element_type=F32)
    hu = jnp.dot(x, wu_ref[...], preferred_element_type=F32)
    h = (hg * _sigmoid(hg)) * hu
    acc[...] += jnp.dot(h.astype(BF16), wd_ref[...], preferred_element_type=F32)

    @pl.when(f == pl.num_programs(1) - 1)
    def _():
        o_ref[...] = _layer_norm(alpha * x_ref[...] + acc[...], g_ref[...], b_ref[...])


def _ffn(x, wg, wu, wd, g, b, *, alpha, tm, tf):
    M, D = x.shape
    F = wg.shape[1]
    full = lambda a: pl.BlockSpec(a.shape, lambda i, f: (0,) * a.ndim)
    return pl.pallas_call(
        functools.partial(_ffn_kernel, alpha=alpha),
        grid=(M // tm, F // tf),
        in_specs=[pl.BlockSpec((tm, D), lambda i, f: (i, 0)),
                  pl.BlockSpec((D, tf), lambda i, f: (0, f)),
                  pl.BlockSpec((D, tf), lambda i, f: (0, f)),
                  pl.BlockSpec((tf, D), lambda i, f: (f, 0)),
                  full(g), full(b)],
        out_specs=pl.BlockSpec((tm, D), lambda i, f: (i, 0)),
        out_shape=jax.ShapeDtypeStruct((M, D), F32),
        scratch_shapes=[pltpu.VMEM((tm, D), BF16), pltpu.VMEM((tm, D), F32)],
        compiler_params=_cparams("parallel", "arbitrary"),
        name="ffn_ln",
    )(x, wg, wu, wd, g, b)


def _router_kernel(x_ref, w_ref, o_ref, *, n_experts):
    logits = jnp.dot(x_ref[...], w_ref[...], precision=HI, preferred_element_type=F32)
    col = lax.broadcasted_iota(jnp.int32, logits.shape, 1)
    neg = jnp.float32(-jnp.inf)
    logits = jnp.where(col < n_experts, logits, neg)
    big = jnp.int32(logits.shape[1])
    m1 = jnp.max(logits, axis=-1, keepdims=True)
    i1 = jnp.min(jnp.where(logits == m1, col, big), axis=-1, keepdims=True)
    rest = jnp.where(col == i1, neg, logits)
    m2 = jnp.max(rest, axis=-1, keepdims=True)
    i2 = jnp.min(jnp.where(rest == m2, col, big), axis=-1, keepdims=True)
    e2 = jnp.exp(m2 - m1)
    p1 = 1.0 / (1.0 + e2)
    p2 = e2 / (1.0 + e2)
    o_ref[...] = (jnp.where(col == 0, p1, 0.0) + jnp.where(col == 1, p2, 0.0)
                  + jnp.where(col == 2, i1.astype(F32), 0.0)
                  + jnp.where(col == 3, i2.astype(F32), 0.0))


def _router(x, w_router_p, *, n_experts, tm):
    M, D = x.shape
    NP = w_router_p.shape[1]
    return pl.pallas_call(
        functools.partial(_router_kernel, n_experts=n_experts),
        grid=(M // tm,),
        in_specs=[pl.BlockSpec((tm, D), lambda i: (i, 0)),
                  pl.BlockSpec((D, NP), lambda i: (0, 0))],
        out_specs=pl.BlockSpec((tm, NP), lambda i: (i, 0)),
        out_shape=jax.ShapeDtypeStruct((M, NP), F32),
        compiler_params=_cparams("parallel"),
        name="router_top2",
    )(x, w_router_p)


def _experts_kernel(te_ref, tv_ref, src_ref, dst_ref, x_hbm, wg_ref, wu_ref, wd_ref,
                    out_hbm, xb, acc, gsem, ssem, *, n_ff, sub):
    i = pl.program_id(0)
    f = pl.program_id(1)
    S = acc.shape[0]
    tf = wg_ref.shape[1]
    n_valid = tv_ref[i]
    active = n_valid > 0
    base = i * S
    subs = [(s * sub, slice(s * sub, (s + 1) * sub)) for s in range(S // sub)]

    def gather_row(r, tok):
        return pltpu.make_async_copy(x_hbm.at[pl.ds(tok, 1)], acc.at[pl.ds(r, 1)], gsem)

    def scatter_row(r, dst):
        return pltpu.make_async_copy(acc.at[pl.ds(r, 1)], out_hbm.at[pl.ds(dst, 1)], ssem)

    def for_rows(n, fn):
        def body(g, _):
            for j in range(SUBLANES):
                fn(g * SUBLANES + j)
            return 0
        lax.fori_loop(0, n // SUBLANES, body, 0)

    @pl.when(jnp.logical_and(active, f == 0))
    def _():
        n_rows = sum(jnp.where(lo < n_valid, sub, 0) for lo, _ in subs)
        for_rows(n_rows, lambda r: gather_row(r, src_ref[base + r]).start())
        for_rows(n_rows, lambda r: gather_row(r, 0).wait())
        for lo, rows in subs:
            @pl.when(lo < n_valid)
            def _():
                xb[rows] = acc[rows].astype(BF16)
                acc[rows] = jnp.zeros((sub, acc.shape[1]), F32)

    @pl.when(active)
    def _():
        valid = n_ff - f * tf
        col = lax.broadcasted_iota(jnp.int32, (1, tf), 1)
        row = lax.broadcasted_iota(jnp.int32, (tf, 1), 0)
        wg = wg_ref[...].astype(BF16)
        wu = wu_ref[...].astype(BF16)
        wd = jnp.where(row < valid, wd_ref[...], 0.0).astype(BF16)
        for lo, rows in subs:
            @pl.when(lo < n_valid)
            def _():
                x = xb[rows]
                hg = jnp.dot(x, wg, preferred_element_type=F32)
                hu = jnp.dot(x, wu, preferred_element_type=F32)
                h = jnp.where(col < valid, (hg * _sigmoid(hg)) * hu, 0.0)
                acc[rows] += jnp.dot(h.astype(BF16), wd, preferred_element_type=F32)

    @pl.when(jnp.logical_and(active, f == pl.num_programs(1) - 1))
    def _():
        def start(r):
            @pl.when(r < n_valid)
            def _():
                scatter_row(r, dst_ref[base + r]).start()

        def wait(r):
            @pl.when(r < n_valid)
            def _():
                scatter_row(r, 0).wait()

        n8 = ((n_valid + SUBLANES - 1) // SUBLANES) * SUBLANES
        for_rows(n8, start)
        for_rows(n8, wait)


def _experts(x, slots, exp_wg, exp_wu, exp_wd, *, layer, sub, n_sub, tf):
    M, D = x.shape
    _, E, _, F = exp_wg.shape
    n_slots = 2 * M
    tm = sub * n_sub
    max_tiles = n_slots // tm + E
    n_f = pl.cdiv(F, tf)

    es = slots[:, 2:4].astype(jnp.int32).reshape(n_slots)
    oh = (es[:, None] == jnp.arange(E, dtype=jnp.int32)[None, :]).astype(jnp.int32)
    csum = jnp.cumsum(oh, axis=0)
    rank = jnp.sum((csum - oh) * oh, axis=1)
    counts = csum[-1]
    tiles = (counts + tm - 1) // tm
    tile_end = jnp.cumsum(tiles)
    tile_start = tile_end - tiles
    pos = tile_start[es] * tm + rank
    sid = jnp.arange(n_slots, dtype=jnp.int32)
    tok = sid // 2
    src = jnp.zeros((max_tiles * tm,), jnp.int32).at[pos].set(tok)
    dst = jnp.zeros((max_tiles * tm,), jnp.int32).at[pos].set((sid % 2) * M + tok)
    tid = jnp.arange(max_tiles, dtype=jnp.int32)
    te = jnp.searchsorted(tile_end, jnp.minimum(tid, tile_end[-1] - 1), side="right")
    te = te.astype(jnp.int32)
    tv = jnp.clip(counts[te] - (tid - tile_start[te]) * tm, 0, tm)
    tv = jnp.where(tid < tile_end[-1], tv, 0).astype(jnp.int32)

    f_idx = lambda i, f, tv: jnp.where(tv[i] > 0, f, n_f - 1)
    up_map = lambda i, f, te, tv, s, d: (layer, te[i], 0, f_idx(i, f, tv))
    down_map = lambda i, f, te, tv, s, d: (layer, te[i], f_idx(i, f, tv), 0)
    grid_spec = pltpu.PrefetchScalarGridSpec(
        num_scalar_prefetch=4,
        grid=(max_tiles, n_f),
        in_specs=[pl.BlockSpec(memory_space=pl.ANY),
                  pl.BlockSpec((None, None, D, tf), up_map),
                  pl.BlockSpec((None, None, D, tf), up_map),
                  pl.BlockSpec((None, None, tf, D), down_map)],
        out_specs=pl.BlockSpec(memory_space=pl.ANY),
        scratch_shapes=[pltpu.VMEM((tm, D), BF16), pltpu.VMEM((tm, D), F32),
                        pltpu.SemaphoreType.DMA(()), pltpu.SemaphoreType.DMA(())],
    )
    return pl.pallas_call(
        functools.partial(_experts_kernel, n_ff=F, sub=sub),
        grid_spec=grid_spec,
        out_shape=jax.ShapeDtypeStruct((n_slots, D), F32),
        compiler_params=pltpu.CompilerParams(dimension_semantics=("arbitrary", "arbitrary"),
                                             vmem_limit_bytes=EXPERT_VMEM_LIMIT),
        name="moe_experts",
    )(te, tv, src, dst, x, exp_wg, exp_wu, exp_wd)


def _combine_kernel(x_ref, s_ref, o0_ref, o1_ref, g_ref, b_ref, o_ref, *, alpha):
    s = s_ref[...]
    f = s[:, 0:1] * o0_ref[...] + s[:, 1:2] * o1_ref[...]
    o_ref[...] = _layer_norm(alpha * x_ref[...] + f, g_ref[...], b_ref[...])


def _combine(x, slots, out2, g, b, *, alpha, tm):
    M, D = x.shape
    NP = slots.shape[1]
    nb = M // tm
    full = lambda a: pl.BlockSpec(a.shape, lambda i: (0,) * a.ndim)
    return pl.pallas_call(
        functools.partial(_combine_kernel, alpha=alpha),
        grid=(nb,),
        in_specs=[pl.BlockSpec((tm, D), lambda i: (i, 0)),
                  pl.BlockSpec((tm, NP), lambda i: (i, 0)),
                  pl.BlockSpec((tm, D), lambda i: (i, 0)),
                  pl.BlockSpec((tm, D), lambda i: (i + nb, 0)),
                  full(g), full(b)],
        out_specs=pl.BlockSpec((tm, D), lambda i: (i, 0)),
        out_shape=jax.ShapeDtypeStruct((M, D), F32),
        compiler_params=_cparams("parallel"),
        name="moe_combine_ln",
    )(x, slots, out2, out2, g, b)


def _pad_to(a, axis, mult):
    n = a.shape[axis]
    pad = (-n) % mult
    if pad == 0:
        return a
    widths = [(0, 0)] * a.ndim
    widths[axis] = (0, pad)
    return jnp.pad(a, widths)


def _block_diag(w, group):
    n, d, e = w.shape
    wg = w.reshape(n // group, group, d, e)
    eye = jnp.eye(group, dtype=w.dtype)
    return jnp.einsum('gqde,qr->gqdre', wg, eye).reshape(n // group, group * d, group * e)


def kernel(x_prompt, x_sample, state_lru_conv, state_lru_h, state_rwkv_shift, state_rwkv_wkv,
           w_in, conv_w, conv_b, lru_wa, lru_ba, lru_wx, lru_bx, lru_lambda,
           mu_shift, w0, w2, a0, a2, g2, k_k, k_a, r_k, gn_g, gn_b, w_o,
           ln1_g, ln1_b, ln2_g, ln2_b, ffn_wg, ffn_wu, ffn_wd,
           w_router, exp_wg, exp_wu, exp_wd):
    Bp, Tp, D = x_prompt.shape
    Bs, Ts, _ = x_sample.shape
    depth = w_in.shape[0]
    W = conv_w.shape[2]
    n_rkv = 3 * W
    n_main = 2 * W + n_rkv
    n_lr = w_in.shape[2] - n_main
    r_decay, r_aaa = w2.shape[1], a2.shape[1]
    n_lr_p = n_lr + (-n_lr) % LANES
    alpha = (2.0 * depth) ** 0.25
    lead = SAMPLE_TP - Ts
    Mp, Ms = Bp * Tp, Bs * Ts
    group = 2 * LANES // HEAD_DIM

    row = lambda a: a.reshape(1, -1)
    head_ones = jnp.kron(jnp.eye(W // HEAD_DIM, dtype=F32),
                         jnp.ones((HEAD_DIM, HEAD_DIM), F32)).astype(BF16)
    x = jnp.concatenate([x_prompt.reshape(Mp, D), x_sample.reshape(Ms, D)], axis=0)
    outs = {k: [] for k in ("pc", "ph", "ps", "pw", "sc", "sh", "ss", "sw")}

    zs = lambda *shape: jnp.zeros((depth, Bs) + shape, F32)
    shift_rows = lambda a: jnp.concatenate([zs(lead - 1, a.shape[-1]), a[:, :, None]], axis=2)
    st_main = jnp.concatenate(
        [jnp.concatenate([zs(lead - (CONV_W - 1), W), state_lru_conv], axis=2),
         zs(lead, W), shift_rows(state_rwkv_shift[:, :, :n_rkv])], axis=3)
    st_lr = shift_rows(_pad_to(state_rwkv_shift[:, :, n_rkv:], 2, LANES))
    h0_ext = jnp.concatenate([zs(lead - 1, W), state_lru_h[:, :, None], zs(Ts, W)], axis=2)
    h0_ext = h0_ext.reshape(depth, Bs * SAMPLE_TP, W)

    for l in range(depth):
        w_main = w_in[l, :, :n_main].astype(BF16)
        w_lr = _pad_to(w_in[l, :, n_main:], 1, LANES).astype(BF16)
        tm_proj = _row_tile(Mp + Ms, PROJ_ROWS)
        proj = _matmul(x, w_main, tm_proj, W)
        proj_lr = _matmul(x, w_lr, tm_proj, n_lr_p)

        proj_s = jnp.concatenate([st_main[l], proj[Mp:].reshape(Bs, Ts, n_main)], axis=1)
        proj_s = proj_s.reshape(Bs * SAMPLE_TP, n_main)
        lr_s = jnp.concatenate([st_lr[l], proj_lr[Mp:].reshape(Bs, Ts, n_lr_p)], axis=1)
        lr_s = lr_s.reshape(Bs * SAMPLE_TP, n_lr_p)

        wbd = jnp.concatenate([_block_diag(lru_wa[l], group), _block_diag(lru_wx[l], group)],
                              axis=2).astype(BF16)
        bias = jnp.stack([lru_ba[l], lru_bx[l]])
        lru_args = (conv_w[l], row(conv_b[l]), wbd, bias, row(lru_lambda[l]))
        lru_p, xtail, htail = _lru(proj, None, *lru_args, rows=Mp,
                                   blocks_per_seq=Tp // TIME_ROWS, tp=Tp, lead=0)
        lru_s, h_s = _lru(proj_s, h0_ext[l], *lru_args, rows=Bs * SAMPLE_TP, blocks_per_seq=1,
                          tp=SAMPLE_TP, lead=lead)

        mu = mu_shift[l, :n_rkv].reshape(3, W)
        mulr = row(_pad_to(mu_shift[l, n_rkv:], 0, LANES))
        zeros_lr = lambda n: jnp.zeros((n, W), F32)
        w2p = jnp.concatenate([w2[l], zeros_lr(n_lr_p - r_decay)], axis=0).astype(BF16)
        a2p = jnp.concatenate([zeros_lr(r_decay), a2[l], zeros_lr(n_lr_p - r_decay - r_aaa)],
                              axis=0).astype(BF16)
        g2p = jnp.concatenate([zeros_lr(r_decay + r_aaa), g2[l], zeros_lr(n_lr_p - n_lr)],
                              axis=0).astype(BF16)
        prep_params = (mu, mulr, row(w0[l]), row(a0[l]), row(k_k[l]), row(k_a[l]),
                       w2p, a2p, g2p, head_ones)
        ops_p, tails = _prep(proj, proj_lr, prep_params, rows=Mp, col0=2,
                             blocks_per_seq=Tp // TIME_ROWS, tp=Tp, lead=0, chunk=WKV_CHUNK)
        ops_s, _ = _prep(proj_s, lr_s, prep_params, rows=Bs * SAMPLE_TP, col0=2,
                         blocks_per_seq=1, tp=SAMPLE_TP, lead=lead, chunk=SAMPLE_TP)
        head_args = (row(r_k[l]), row(gn_g[l]), row(gn_b[l]))
        rwkv_p, S_p = _wkv_prompt(ops_p, *head_args, n_seq=Bp, tp=Tp, rows_blk=WKV_ROWS,
                                  chunk=WKV_CHUNK, width=min(W, WKV_WIDTH))
        rwkv_s, S_s = _wkv_sample(ops_s, state_rwkv_wkv[l], *head_args, n_seq=Bs, tp=SAMPLE_TP,
                                  rows_blk=WKV_SAMPLE_ROWS, width=min(W, WKV_SAMPLE_WIDTH),
                                  group=WKV_SAMPLE_GROUP)

        real = lambda a: a.reshape(Bs, SAMPLE_TP, -1)[:, lead:].reshape(Ms, -1)
        lru_out = jnp.concatenate([lru_p, real(lru_s)], axis=0)
        rwkv_out = jnp.concatenate([rwkv_p, real(rwkv_s)], axis=0)
        x = _oproj(x, lru_out, rwkv_out, w_o[l].astype(BF16), row(ln1_g[l]), row(ln1_b[l]),
                   alpha=alpha, tm=256)

        outs["pc"].append(xtail[:, SUBLANES - (CONV_W - 1):])
        outs["ph"].append(htail[:, 0])
        outs["ps"].append(jnp.concatenate([t[:, -1] for t in tails[:3]]
                                          + [tails[3][:, -1, :n_lr]], axis=-1))
        outs["pw"].append(S_p)
        sm = proj_s.reshape(Bs, SAMPLE_TP, n_main)
        outs["sc"].append(sm[:, SAMPLE_TP - (CONV_W - 1):, :W])
        outs["sh"].append(h_s.reshape(Bs, SAMPLE_TP, W)[:, -1])
        outs["ss"].append(jnp.concatenate([sm[:, -1, 2 * W:],
                                           lr_s.reshape(Bs, SAMPLE_TP, n_lr_p)[:, -1, :n_lr]],
                                          axis=-1))
        outs["sw"].append(S_s)

        j = l // 2
        if l % 2 == 0:
            wg = _pad_to(ffn_wg[j], 1, FF_TILE).astype(BF16)
            wu = _pad_to(ffn_wu[j], 1, FF_TILE).astype(BF16)
            wd = _pad_to(ffn_wd[j], 0, FF_TILE).astype(BF16)
            x = _ffn(x, wg, wu, wd, row(ln2_g[l]), row(ln2_b[l]),
                     alpha=alpha, tm=TOKEN_TILE, tf=FF_TILE)
        else:
            slots = _router(x, _pad_to(w_router[j], 1, LANES), n_experts=exp_wg.shape[1],
                            tm=TOKEN_TILE)
            out2 = _experts(x, slots, exp_wg, exp_wu, exp_wd, layer=j, sub=EXPERT_SUB,
                            n_sub=EXPERT_SUBS, tf=EXPERT_FF_TILE)
            x = _combine(x, slots, out2, row(ln2_g[l]), row(ln2_b[l]), alpha=alpha, tm=TOKEN_TILE)

    y_prompt = x[:Mp].reshape(Bp, Tp, D)
    y_sample = x[Mp:].reshape(Bs, Ts, D)
    st = lambda k: jnp.stack(outs[k])
    return (y_prompt, y_sample, st("pc"), st("ph"), st("ps"), st("pw"),
            st("sc"), st("sh"), st("ss"), st("sw"))
```

```python
import functools

import jax
import jax.numpy as jnp
from jax import lax
from jax.experimental import pallas as pl
from jax.experimental.pallas import tpu as pltpu

F32 = jnp.float32
BF16 = jnp.bfloat16
HI = lax.Precision.HIGHEST

HEAD_DIM = 64
CONV_W = 4
LRU_C = 8.0
N_EXPERTS = 8
LN_EPS = 1e-5
GN_EPS = 64e-5

LANES = 128
SUBLANES = 8
VMEM_LIMIT = 48 * 1024 * 1024
SAMPLE_TP = 8
TIME_ROWS = 256
TOKEN_TILE = 512
PROJ_ROWS = 1088
FF_TILE = 512
EXPERT_SUB = 768
EXPERT_SUBS = 3
EXPERT_GRAIN = 256
EXPERT_FF_TILE = 256
EXPERT_VMEM_LIMIT = 56 * 1024 * 1024
WKV_CHUNK = 64
WKV_ROWS = 256
WKV_WIDTH = 1024
PRE_CHUNKS = 2
WKV_SAMPLE_ROWS = 128
WKV_SAMPLE_WIDTH = 256
WKV_SAMPLE_GROUP = 8


def _cparams(*sem):
    return pltpu.CompilerParams(dimension_semantics=sem, vmem_limit_bytes=VMEM_LIMIT)


def _sigmoid(x):
    return 1.0 / (1.0 + jnp.exp(-x))


def _softplus(x):
    return jnp.maximum(x, 0.0) + jnp.log1p(jnp.exp(-jnp.abs(x)))


def _gelu_tanh(x):
    c = 0.7978845608028654
    return 0.5 * x * (1.0 + jnp.tanh(c * (x + 0.044715 * (x * x * x))))


def _layer_norm(x, g, b):
    mu = jnp.mean(x, axis=-1, keepdims=True)
    xc = x - mu
    var = jnp.mean(xc * xc, axis=-1, keepdims=True)
    return xc * lax.rsqrt(var + LN_EPS) * g + b


def _shift_rows(x, carry, k):
    xr = pltpu.roll(x, k, 0)
    cr = pltpu.roll(carry, k, 0)
    row = lax.broadcasted_iota(jnp.int32, carry.shape, 0)
    first = jnp.where(row < k, cr, xr[:SUBLANES])
    return jnp.concatenate([first, xr[SUBLANES:]], axis=0)


def _row_tile(m, target):
    return max(t for t in range(SUBLANES, target + 1, SUBLANES) if m % t == 0)


def _mm_kernel(x_ref, w_ref, o_ref, xb):
    @pl.when(pl.program_id(1) == 0)
    def _():
        xb[...] = x_ref[...].astype(BF16)

    o_ref[...] = jnp.dot(xb[...], w_ref[...], preferred_element_type=F32)


def _matmul(x, w, tm, tn):
    M, K = x.shape
    N = w.shape[1]
    return pl.pallas_call(
        _mm_kernel,
        grid=(M // tm, N // tn),
        in_specs=[pl.BlockSpec((tm, K), lambda i, j: (i, 0)),
                  pl.BlockSpec((K, tn), lambda i, j: (0, j))],
        out_specs=pl.BlockSpec((tm, tn), lambda i, j: (i, j)),
        out_shape=jax.ShapeDtypeStruct((M, N), F32),
        scratch_shapes=[pltpu.VMEM((tm, K), BF16)],
        compiler_params=_cparams("parallel", "arbitrary"),
        name="in_proj",
    )(x, w)


def _lru_kernel(*refs, blocks_per_seq, tp, lead):
    if lead:
        (xb_ref, gate_ref, h0_ref, cw_ref, cb_ref, wbd_ref, bias_ref, lam_ref,
         out_ref, h_ref, xcarry, hcarry, a_s, u_s) = refs
    else:
        (xb_ref, gate_ref, cw_ref, cb_ref, wbd_ref, bias_ref, lam_ref,
         out_ref, xtail_ref, htail_ref, xcarry, hcarry, a_s, u_s, h_ref) = refs
    R, W = xb_ref.shape
    i = pl.program_id(0)

    @pl.when(i % blocks_per_seq == 0)
    def _():
        xcarry[...] = jnp.zeros_like(xcarry)
        hcarry[...] = jnp.zeros_like(hcarry)

    xb = xb_ref[...]
    carry = xcarry[...]
    cw = cw_ref[...]
    xc = cb_ref[...] + cw[3:4] * xb
    for k in range(1, CONV_W):
        xc = xc + cw[3 - k:4 - k] * _shift_rows(xb, carry, k)
    xcarry[...] = xb[R - SUBLANES:]

    xcb = xc.astype(BF16)
    nb = wbd_ref.shape[0]
    bw = W // nb
    pre_a, pre_x = [], []
    for g in range(nb):
        pre = jnp.dot(xcb[:, g * bw:(g + 1) * bw], wbd_ref[g], preferred_element_type=F32)
        pre_a.append(pre[:, :bw])
        pre_x.append(pre[:, bw:])
    bias = bias_ref[...]
    r = _sigmoid(jnp.concatenate(pre_a, axis=1) + bias[0:1])
    ig = _sigmoid(jnp.concatenate(pre_x, axis=1) + bias[1:2])
    log_a = -LRU_C * r * _softplus(-lam_ref[...])
    a = jnp.exp(log_a)
    u = jnp.sqrt(-jnp.tanh(log_a) * (a * a + 1.0)) * (ig * xc)
    if lead:
        row = lax.broadcasted_iota(jnp.int32, (R, W), 0)
        is_state = (row % tp) == (lead - 1)
        a = jnp.where(is_state, 0.0, a)
        u = jnp.where(is_state, h0_ref[...], u)
    a_s[...] = a
    u_s[...] = u

    def tile_body(t, h):
        off = pl.multiple_of(t * SUBLANES, SUBLANES)
        a8 = a_s[pl.ds(off, SUBLANES), :]
        u8 = u_s[pl.ds(off, SUBLANES), :]
        rows = []
        for j in range(SUBLANES):
            h = a8[j:j + 1] * h + u8[j:j + 1]
            rows.append(h)
        h_ref[pl.ds(off, SUBLANES), :] = jnp.concatenate(rows, axis=0)
        return h

    h_last = lax.fori_loop(0, R // SUBLANES, tile_body, hcarry[0:1])
    hcarry[...] = jnp.broadcast_to(h_last, hcarry.shape)
    out_ref[...] = h_ref[...] * _gelu_tanh(gate_ref[...])
    if not lead:
        xtail_ref[...] = xb[R - SUBLANES:]
        htail_ref[...] = hcarry[...]


def _lru(proj, h0_ext, cw, cb, wbd, bias, lam, *, rows, blocks_per_seq, tp, lead):
    W = cw.shape[1]
    R = TIME_ROWS
    row_blk = lambda c: pl.BlockSpec((R, W), lambda i, c=c: (i, c))
    full = lambda a: pl.BlockSpec(a.shape, lambda i: (0,) * a.ndim)
    ins = [proj, proj]
    specs = [row_blk(0), row_blk(1)]
    scratch = [pltpu.VMEM((SUBLANES, W), F32), pltpu.VMEM((SUBLANES, W), F32),
               pltpu.VMEM((R, W), F32), pltpu.VMEM((R, W), F32)]
    if lead:
        ins.append(h0_ext)
        specs.append(row_blk(0))
        out_specs = [row_blk(0), row_blk(0)]
        out_shape = [jax.ShapeDtypeStruct((rows, W), F32)] * 2
    else:
        n_seq = rows // (R * blocks_per_seq)
        tail = pl.BlockSpec((None, SUBLANES, W), lambda i: (i // blocks_per_seq, 0, 0))
        out_specs = [row_blk(0), tail, tail]
        out_shape = ([jax.ShapeDtypeStruct((rows, W), F32)]
                     + [jax.ShapeDtypeStruct((n_seq, SUBLANES, W), F32)] * 2)
        scratch.append(pltpu.VMEM((R, W), F32))
    params = [cw, cb, wbd, bias, lam]
    ins += params
    specs += [full(p) for p in params]
    return pl.pallas_call(
        functools.partial(_lru_kernel, blocks_per_seq=blocks_per_seq, tp=tp, lead=lead),
        grid=(rows // R,),
        in_specs=specs,
        out_specs=out_specs,
        out_shape=out_shape,
        scratch_shapes=scratch,
        compiler_params=_cparams("arbitrary"),
        name="rg_lru",
    )(*ins)


def _prep_kernel(r_ref, k_ref, v_ref, lr_ref, mu_ref, mulr_ref, w0_ref, a0_ref, kkw_ref, kaw_ref,
                 w2_ref, a2_ref, g2_ref, ones_ref,
                 ro_ref, km_ref, vo_ref, lw_ref, cum_ref, kk_ref, bk_ref, g_ref,
                 rt_ref, kt_ref, vt_ref, lt_ref,
                 cr, ck, cv, clr, *, blocks_per_seq, tp, lead, chunk):
    i = pl.program_id(0)

    @pl.when(i % blocks_per_seq == 0)
    def _():
        for c in (cr, ck, cv, clr):
            c[...] = jnp.zeros_like(c)

    def token_shift(x_ref, c_ref, mu):
        x = x_ref[...]
        prev = _shift_rows(x, c_ref[...], 1)
        c_ref[...] = x[x.shape[0] - SUBLANES:]
        return x + (prev - x) * mu

    mu = mu_ref[...]
    r = token_shift(r_ref, cr, mu[0:1])
    k = token_shift(k_ref, ck, mu[1:2])
    v = token_shift(v_ref, cv, mu[2:3])
    xl = token_shift(lr_ref, clr, mulr_ref[...])
    dw = jnp.dot(jnp.tanh(xl).astype(BF16), w2_ref[...], preferred_element_type=F32)
    w_log = -_softplus(-(w0_ref[...] + dw)) - 0.5
    lw = -jnp.exp(w_log)
    da = jnp.dot(xl.astype(BF16), a2_ref[...], preferred_element_type=F32)
    ah = _sigmoid(a0_ref[...] + da)
    g_ref[...] = jnp.dot(_sigmoid(xl).astype(BF16), g2_ref[...],
                         preferred_element_type=F32).astype(g_ref.dtype)

    R = r.shape[0]
    if lead:
        t = lax.broadcasted_iota(jnp.int32, r.shape, 0) % tp
        real = t >= lead
        k = jnp.where(real, k, 0.0)
        v = jnp.where(real, v, 0.0)
        lw = jnp.where(real, lw, 0.0)

    kk = k * kkw_ref[...]
    sq = kk * kk
    sq_hi = sq.astype(BF16)
    sq_lo = (sq - sq_hi.astype(F32)).astype(BF16)
    ss = (jnp.dot(sq_hi, ones_ref[...], preferred_element_type=F32)
          + jnp.dot(sq_lo, ones_ref[...], preferred_element_type=F32))
    kk = kk / jnp.maximum(jnp.sqrt(ss), 1e-12)

    shift = chunk.bit_length() - 1
    rr = lax.broadcasted_iota(jnp.int32, (R, R), 0)
    cc = lax.broadcasted_iota(jnp.int32, (R, R), 1)
    same = jnp.right_shift(rr, shift) == jnp.right_shift(cc, shift)
    tri = jnp.where(jnp.logical_and(same, cc <= rr), 1.0, 0.0)
    cum_ref[...] = jnp.dot(tri, lw, precision=HI, preferred_element_type=F32)

    ro_ref[...] = r.astype(ro_ref.dtype)
    km_ref[...] = (k * (1.0 + (ah - 1.0) * kaw_ref[...])).astype(km_ref.dtype)
    vo_ref[...] = v.astype(vo_ref.dtype)
    lw_ref[...] = lw
    kk_ref[...] = kk.astype(kk_ref.dtype)
    bk_ref[...] = (kk * ah).astype(bk_ref.dtype)
    for t_ref, c_ref in ((rt_ref, cr), (kt_ref, ck), (vt_ref, cv), (lt_ref, clr)):
        t_ref[...] = c_ref[...]


def _prep(proj, proj_lr, params, *, rows, col0, blocks_per_seq, tp, lead, chunk, op_dtype):
    W = params[2].shape[1]
    LR = proj_lr.shape[1]
    R = TIME_ROWS
    n_grp = rows // (R * blocks_per_seq)
    row_blk = lambda c: pl.BlockSpec((R, W), lambda i, c=c: (i, c))
    full = lambda a: pl.BlockSpec(a.shape, lambda i: (0,) * a.ndim)
    tail = lambda n: pl.BlockSpec((None, SUBLANES, n), lambda i: (i // blocks_per_seq, 0, 0))
    outs = pl.pallas_call(
        functools.partial(_prep_kernel, blocks_per_seq=blocks_per_seq, tp=tp, lead=lead,
                          chunk=chunk),
        grid=(rows // R,),
        in_specs=[row_blk(col0), row_blk(col0 + 1), row_blk(col0 + 2),
                  pl.BlockSpec((R, LR), lambda i: (i, 0))] + [full(p) for p in params],
        out_specs=[row_blk(0)] * 8 + [tail(W)] * 3 + [tail(LR)],
        out_shape=([jax.ShapeDtypeStruct((rows, W), dt)
                    for dt in (op_dtype,) * 3 + (F32, F32) + (op_dtype,) * 3]
                   + [jax.ShapeDtypeStruct((n_grp, SUBLANES, W), F32)] * 3
                   + [jax.ShapeDtypeStruct((n_grp, SUBLANES, LR), F32)]),
        scratch_shapes=[pltpu.VMEM((SUBLANES, W), F32)] * 3 + [pltpu.VMEM((SUBLANES, LR), F32)],
        compiler_params=_cparams("arbitrary"),
        name="rwkv_prep",
    )(proj, proj, proj, proj_lr, *params)
    return outs[:8], outs[8:]


def _mm(a, b):
    return jnp.dot(a.astype(BF16), b.astype(BF16), preferred_element_type=F32)


def _mm_nt(a, b):
    return lax.dot_general(a.astype(BF16), b.astype(BF16), (((1,), (1,)), ((), ())),
                           preferred_element_type=F32)


def _mm_tn(a, b):
    return lax.dot_general(a.astype(BF16), b.astype(BF16), (((0,), (0,)), ((), ())),
                           preferred_element_type=F32)


def _pairs_pre(tiles):
    C = tiles[0][0].shape[0]
    N = HEAD_DIM
    n = range(len(tiles))
    h0 = lax.broadcasted_iota(jnp.int32, (C, LANES), 1) < N
    stack = lambda x: jnp.concatenate([jnp.where(h0, x, 0.0), jnp.where(h0, 0.0, x)], axis=0)
    ri = lax.broadcasted_iota(jnp.int32, (2 * C, 2 * C), 0)
    ci = lax.broadcasted_iota(jnp.int32, (2 * C, 2 * C), 1)
    strict = (ci & (C - 1)) < (ri & (C - 1))
    incl = (ci & (C - 1)) <= (ri & (C - 1))
    eye = jnp.where(ri == ci, 1.0, 0.0)

    lhs, rhs, vs, zs, gl = [], [], [], [], []
    for r, kmod, v, lw, cum, kk, bk in tiles:
        g_incl = jnp.exp(cum)
        g_excl = jnp.exp(cum - lw)
        g_inv = jnp.exp(-cum)
        g_rest = jnp.exp(cum[C - 1:C] - cum)
        lhs.append(jnp.concatenate([stack(-(kk * g_excl)), stack(r * g_incl)],
                                   axis=0).astype(BF16))
        rhs.append(jnp.concatenate([stack(bk * g_inv), stack(kmod * g_inv)],
                                   axis=0).astype(BF16))
        vs.append(stack(v).astype(BF16))
        zs.append(jnp.concatenate([stack(bk * g_rest), stack(kmod * g_rest)],
                                  axis=0).astype(BF16))
        gl.append(g_incl[C - 1:C])

    aa = [_mm_nt(lhs[i], rhs[i]) for i in n]
    a_ab = [jnp.where(strict, aa[i][:2 * C, :2 * C], 0.0) for i in n]
    a_ak = [jnp.where(strict, aa[i][:2 * C, 2 * C:], 0.0).astype(BF16) for i in n]
    arbk = [jnp.concatenate([jnp.where(incl, aa[i][2 * C:, :2 * C], 0.0),
                             jnp.where(incl, aa[i][2 * C:, 2 * C:], 0.0)],
                            axis=1).astype(BF16) for i in n]
    av = [_mm(a_ak[i], vs[i]) for i in n]

    t = [eye + a_ab[i] for i in n]
    n_iter = max(1, (C - 1).bit_length())
    if n_iter > 1:
        p = [_mm(a_ab[i], a_ab[i]) for i in n]
    for it in range(1, n_iter):
        if it + 1 < n_iter:
            pt = [_mm(p[i], jnp.concatenate([t[i], p[i]], axis=1)) for i in n]
            t = [t[i] + pt[i][:, :2 * C] for i in n]
            p = [pt[i][:, 2 * C:] for i in n]
        else:
            t = [t[i] + _mm(p[i], t[i]) for i in n]
    t = [t[i].astype(BF16) for i in n]
    tav = [_mm(t[i], av[i]) for i in n]
    return [(lhs[i], t[i], tav[i], arbk[i], vs[i], zs[i], gl[i]) for i in n]


def _pairs_step(states, pres):
    n = range(len(states))
    C2 = pres[0][1].shape[0]
    ls = [_mm_nt(pres[i][0], states[i]) for i in n]
    u = [_mm(pres[i][1], ls[i][:C2]) + pres[i][2] for i in n]
    xs = [jnp.concatenate([u[i].astype(BF16), pres[i][4]], axis=0) for i in n]
    s1 = [states[i] * pres[i][6] + _mm_tn(xs[i], pres[i][5]) for i in n]
    ys = [ls[i][C2:] + _mm(pres[i][3], xs[i]) for i in n]
    return [(ys[i][:C2 // 2] + ys[i][C2 // 2:], s1[i]) for i in n]


def _wkv_epilogue(y, r, kmod, v, g, rk, gg, gb):
    N = HEAD_DIM
    h0 = lax.broadcasted_iota(jnp.int32, y.shape, 1) < N

    def head_sum(x):
        s0 = jnp.sum(jnp.where(h0, x, 0.0), axis=-1, keepdims=True)
        s1 = jnp.sum(jnp.where(h0, 0.0, x), axis=-1, keepdims=True)
        return jnp.where(h0, s0, s1)

    yc = y - head_sum(y) * (1.0 / N)
    var = head_sum(yc * yc) * (1.0 / N)
    yn = yc * lax.rsqrt(var + GN_EPS) * gg + gb
    bonus = head_sum(r * kmod * rk) * v
    return (yn + bonus) * g


def _wkv_prompt_kernel(r_ref, km_ref, v_ref, lw_ref, cum_ref, kk_ref, bk_ref, g_ref,
                       rk_ref, gg_ref, gb_ref, o_ref, so_ref,
                       st, lhs_s, t_s, tav_s, arbk_s, vs_s, zs_s, gl_s, *, chunk):
    N = HEAD_DIM
    C = chunk
    rows_blk, width = r_ref.shape
    pairs = width // LANES
    n_chunks = rows_blk // C
    tb = pl.program_id(2)

    @pl.when(tb == 0)
    def _():
        st[...] = jnp.zeros_like(st)

    in_refs = (r_ref, km_ref, v_ref, lw_ref, cum_ref, kk_ref, bk_ref)
    pre_refs = (lhs_s, t_s, tav_s, arbk_s, vs_s, zs_s)
    lanes = [slice(p * LANES, (p + 1) * LANES) for p in range(pairs)]

    def pre_body(c2, _):
        where = [(c2 * PRE_CHUNKS + j, p) for j in range(PRE_CHUNKS) for p in range(pairs)]
        tiles = []
        for c, p in where:
            rows = pl.ds(pl.multiple_of(c * C, C), C)
            tiles.append(tuple(ref[rows, lanes[p]].astype(F32) for ref in in_refs))
        for (c, p), pre in zip(where, _pairs_pre(tiles)):
            for ref, val in zip(pre_refs, pre[:6]):
                ref[c, p] = val
            gl_s[c, p] = jnp.broadcast_to(pre[6], (SUBLANES, LANES))
        return 0

    lax.fori_loop(0, n_chunks // PRE_CHUNKS, pre_body, 0)

    def step_body(c, _):
        rows = pl.ds(pl.multiple_of(c * C, C), C)
        pres = [tuple(ref[c, p] for ref in pre_refs) + (gl_s[c, p][0:1],) for p in range(pairs)]
        outs = _pairs_step([st[p] for p in range(pairs)], pres)
        for p, (y, s1) in enumerate(outs):
            st[p] = s1
            r, kmod, v, g = (ref[rows, lanes[p]].astype(F32)
                             for ref in (r_ref, km_ref, v_ref, g_ref))
            o_ref[rows, lanes[p]] = _wkv_epilogue(
                y, r, kmod, v, g, rk_ref[:, lanes[p]], gg_ref[:, lanes[p]], gb_ref[:, lanes[p]])
        return 0

    lax.fori_loop(0, n_chunks, step_body, 0)

    @pl.when(tb == pl.num_programs(2) - 1)
    def _():
        for p in range(pairs):
            sb = st[p]
            so_ref[0, 2 * p] = sb[:N, :N]
            so_ref[0, 2 * p + 1] = sb[N:, N:]


def _wkv_sample_kernel(r_ref, km_ref, v_ref, lw_ref, cum_ref, kk_ref, bk_ref, g_ref, s0_ref,
                       rk_ref, gg_ref, gb_ref, o_ref, so_ref, *, tp, group):
    N = HEAD_DIM
    rows_blk, width = r_ref.shape
    pairs = width // LANES
    n_seq = rows_blk // tp
    zero = jnp.zeros((N, N), F32)

    in_refs = (r_ref, km_ref, v_ref, lw_ref, cum_ref, kk_ref, bk_ref)
    lanes = [slice(p * LANES, (p + 1) * LANES) for p in range(pairs)]

    def group_body(gi, _):
        where = [(gi * group + j, p) for j in range(group) for p in range(pairs)]
        tiles, states = [], []
        for s, p in where:
            rows = pl.ds(pl.multiple_of(s * tp, SUBLANES), tp)
            tiles.append(tuple(ref[rows, lanes[p]] for ref in in_refs))
            top = jnp.concatenate([s0_ref[s, 2 * p], zero], axis=1)
            bot = jnp.concatenate([zero, s0_ref[s, 2 * p + 1]], axis=1)
            states.append(jnp.concatenate([top, bot], axis=0))
        outs = _pairs_step(states, _pairs_pre(tiles))
        for (s, p), tile, (y, s1) in zip(where, tiles, outs):
            rows = pl.ds(pl.multiple_of(s * tp, SUBLANES), tp)
            so_ref[s, 2 * p] = s1[:N, :N]
            so_ref[s, 2 * p + 1] = s1[N:, N:]
            o_ref[rows, lanes[p]] = _wkv_epilogue(
                y, tile[0], tile[1], tile[2], g_ref[rows, lanes[p]],
                rk_ref[:, lanes[p]], gg_ref[:, lanes[p]], gb_ref[:, lanes[p]])
        return 0

    lax.fori_loop(0, n_seq // group, group_body, 0)


def _wkv_prompt(ops, rk_w, gg_w, gb_w, *, n_seq, tp, rows_blk, chunk, width):
    rows, W = ops[0].shape
    N = HEAD_DIM
    H = W // N
    hq = width // N
    pairs = width // LANES
    tbs = tp // rows_blk
    n_chunks = rows_blk // chunk
    C2, C4 = 2 * chunk, 4 * chunk
    blk = pl.BlockSpec((rows_blk, width), lambda i, q, t: (i * tbs + t, q))
    pblk = pl.BlockSpec((1, width), lambda i, q, t: (0, q))
    sblk = pl.BlockSpec((1, hq, N, N), lambda i, q, t: (i, q, 0, 0))
    per_chunk = lambda shape, dt: pltpu.VMEM((n_chunks, pairs) + shape, dt)
    return pl.pallas_call(
        functools.partial(_wkv_prompt_kernel, chunk=chunk),
        grid=(n_seq, H // hq, tbs),
        in_specs=[blk] * len(ops) + [pblk] * 3,
        out_specs=[blk, sblk],
        out_shape=[jax.ShapeDtypeStruct((rows, W), F32),
                   jax.ShapeDtypeStruct((n_seq, H, N, N), F32)],
        scratch_shapes=[pltpu.VMEM((pairs, LANES, LANES), F32),
                        per_chunk((C4, LANES), BF16), per_chunk((C2, C2), BF16),
                        per_chunk((C2, LANES), F32), per_chunk((C2, C4), BF16),
                        per_chunk((C2, LANES), BF16), per_chunk((C4, LANES), BF16),
                        per_chunk((SUBLANES, LANES), F32)],
        compiler_params=_cparams("parallel", "parallel", "arbitrary"),
        name="wkv7_prompt",
    )(*ops, rk_w, gg_w, gb_w)


def _wkv_sample(ops, s0, rk_w, gg_w, gb_w, *, n_seq, tp, rows_blk, width, group):
    rows, W = ops[0].shape
    N = HEAD_DIM
    H = W // N
    hq = width // N
    seq_blk = rows_blk // tp
    blk = pl.BlockSpec((rows_blk, width), lambda i, q: (i, q))
    pblk = pl.BlockSpec((1, width), lambda i, q: (0, q))
    sblk = pl.BlockSpec((seq_blk, hq, N, N), lambda i, q: (i, q, 0, 0))
    return pl.pallas_call(
        functools.partial(_wkv_sample_kernel, tp=tp, group=group),
        grid=(n_seq // seq_blk, H // hq),
        in_specs=[blk] * len(ops) + [sblk] + [pblk] * 3,
        out_specs=[blk, sblk],
        out_shape=[jax.ShapeDtypeStruct((rows, W), F32),
                   jax.ShapeDtypeStruct((n_seq, H, N, N), F32)],
        compiler_params=_cparams("parallel", "parallel"),
        name="wkv7_sample",
    )(*ops, s0, rk_w, gg_w, gb_w)


def _oproj_kernel(x_ref, ap_ref, bp_ref, as_ref, bs_ref, w_ref, g_ref, bb_ref, o_ref, *,
                  alpha, n_p):
    half = ap_ref.shape[1]
    first = pl.program_id(0) < n_p
    a = jnp.where(first, ap_ref[...], as_ref[...]).astype(BF16)
    b = jnp.where(first, bp_ref[...], bs_ref[...]).astype(BF16)
    m = jnp.dot(a, w_ref[:half], preferred_element_type=F32)
    m = m + jnp.dot(b, w_ref[half:], preferred_element_type=F32)
    o_ref[...] = _layer_norm(alpha * x_ref[...] + m, g_ref[...], bb_ref[...])


def _oproj(x, lru_p, rwkv_p, lru_s, rwkv_s, w_o, g, b, *, alpha, tm):
    M, D = x.shape
    Wh = lru_p.shape[1]
    n_p = lru_p.shape[0] // tm
    full = lambda a: pl.BlockSpec(a.shape, lambda i: (0,) * a.ndim)
    first = pl.BlockSpec((tm, Wh), lambda i: (jnp.minimum(i, n_p - 1), 0))
    second = pl.BlockSpec((tm, Wh), lambda i: (jnp.maximum(i - n_p, 0), 0))
    return pl.pallas_call(
        functools.partial(_oproj_kernel, alpha=alpha, n_p=n_p),
        grid=(M // tm,),
        in_specs=[pl.BlockSpec((tm, D), lambda i: (i, 0)), first, first, second, second,
                  full(w_o), full(g), full(b)],
        out_specs=pl.BlockSpec((tm, D), lambda i: (i, 0)),
        out_shape=jax.ShapeDtypeStruct((M, D), F32),
        compiler_params=_cparams("parallel"),
        name="out_proj_ln",
    )(x, lru_p, rwkv_p, lru_s, rwkv_s, w_o, g, b)


def _ffn_kernel(x_ref, wg_ref, wu_ref, wd_ref, g_ref, b_ref, o_ref, xb, acc, *, alpha):
    f = pl.program_id(1)

    @pl.when(f == 0)
    def _():
        xb[...] = x_ref[...].astype(BF16)
        acc[...] = jnp.zeros_like(acc)

    x = xb[...]
    hg = jnp.dot(x, wg_ref[...], preferred_element_type=F32)
    hu = jnp.dot(x, wu_ref[...], preferred_element_type=F32)
    h = (hg * _sigmoid(hg)) * hu
    acc[...] += jnp.dot(h.astype(BF16), wd_ref[...], preferred_element_type=F32)

    @pl.when(f == pl.num_programs(1) - 1)
    def _():
        o_ref[...] = _layer_norm(alpha * x_ref[...] + acc[...], g_ref[...], b_ref[...])


def _ffn(x, wg, wu, wd, g, b, *, alpha, tm, tf):
    M, D = x.shape
    F = wg.shape[1]
    full = lambda a: pl.BlockSpec(a.shape, lambda i, f: (0,) * a.ndim)
    return pl.pallas_call(
        functools.partial(_ffn_kernel, alpha=alpha),
        grid=(M // tm, F // tf),
        in_specs=[pl.BlockSpec((tm, D), lambda i, f: (i, 0)),
                  pl.BlockSpec((D, tf), lambda i, f: (0, f)),
                  pl.BlockSpec((D, tf), lambda i, f: (0, f)),
                  pl.BlockSpec((tf, D), lambda i, f: (f, 0)),
                  full(g), full(b)],
        out_specs=pl.BlockSpec((tm, D), lambda i, f: (i, 0)),
        out_shape=jax.ShapeDtypeStruct((M, D), F32),
        scratch_shapes=[pltpu.VMEM((tm, D), BF16), pltpu.VMEM((tm, D), F32)],
        compiler_params=_cparams("parallel", "arbitrary"),
        name="ffn_ln",
    )(x, wg, wu, wd, g, b)


def _router_kernel(x_ref, w_ref, o_ref, *, n_experts):
    logits = jnp.dot(x_ref[...], w_ref[...], precision=HI, preferred_element_type=F32)
    col = lax.broadcasted_iota(jnp.int32, logits.shape, 1)
    neg = jnp.float32(-jnp.inf)
    logits = jnp.where(col < n_experts, logits, neg)
    big = jnp.int32(logits.shape[1])
    m1 = jnp.max(logits, axis=-1, keepdims=True)
    i1 = jnp.min(jnp.where(logits == m1, col, big), axis=-1, keepdims=True)
    rest = jnp.where(col == i1, neg, logits)
    m2 = jnp.max(rest, axis=-1, keepdims=True)
    i2 = jnp.min(jnp.where(rest == m2, col, big), axis=-1, keepdims=True)
    e2 = jnp.exp(m2 - m1)
    p1 = 1.0 / (1.0 + e2)
    p2 = e2 / (1.0 + e2)
    o_ref[...] = (jnp.where(col == 0, p1, 0.0) + jnp.where(col == 1, p2, 0.0)
                  + jnp.where(col == 2, i1.astype(F32), 0.0)
                  + jnp.where(col == 3, i2.astype(F32), 0.0))


def _router(x, w_router_p, *, n_experts, tm):
    M, D = x.shape
    NP = w_router_p.shape[1]
    return pl.pallas_call(
        functools.partial(_router_kernel, n_experts=n_experts),
        grid=(M // tm,),
        in_specs=[pl.BlockSpec((tm, D), lambda i: (i, 0)),
                  pl.BlockSpec((D, NP), lambda i: (0, 0))],
        out_specs=pl.BlockSpec((tm, NP), lambda i: (i, 0)),
        out_shape=jax.ShapeDtypeStruct((M, NP), F32),
        compiler_params=_cparams("parallel"),
        name="router_top2",
    )(x, w_router_p)


def _experts_kernel(te_ref, tv_ref, src_ref, dst_ref, x_hbm, wg_ref, wu_ref, wd_ref,
                    out_hbm, xb, acc, gsem, ssem, *, n_ff, sub):
    i = pl.program_id(0)
    f = pl.program_id(1)
    S = acc.shape[0]
    tf = wg_ref.shape[1]
    n_valid = tv_ref[i]
    active = n_valid > 0
    base = i * S
    subs = [(s * sub, slice(s * sub, (s + 1) * sub)) for s in range(S // sub)]

    def gather_row(r, tok):
        return pltpu.make_async_copy(x_hbm.at[pl.ds(tok, 1)], acc.at[pl.ds(r, 1)], gsem)

    def scatter_row(r, dst):
        return pltpu.make_async_copy(acc.at[pl.ds(r, 1)], out_hbm.at[pl.ds(dst, 1)], ssem)

    def for_rows(n, fn):
        def body(g, _):
            for j in range(SUBLANES):
                fn(g * SUBLANES + j)
            return 0
        lax.fori_loop(0, n // SUBLANES, body, 0)

    @pl.when(jnp.logical_and(active, f == 0))
    def _():
        n_rows = sum(jnp.where(lo < n_valid, sub, 0) for lo, _ in subs)
        for_rows(n_rows, lambda r: gather_row(r, src_ref[base + r]).start())
        for_rows(n_rows, lambda r: gather_row(r, 0).wait())
        for lo, rows in subs:
            @pl.when(lo < n_valid)
            def _():
                xb[rows] = acc[rows].astype(BF16)
                acc[rows] = jnp.zeros((sub, acc.shape[1]), F32)

    @pl.when(active)
    def _():
        valid = n_ff - f * tf
        col = lax.broadcasted_iota(jnp.int32, (1, tf), 1)
        row = lax.broadcasted_iota(jnp.int32, (tf, 1), 0)
        wg = wg_ref[...].astype(BF16)
        wu = wu_ref[...].astype(BF16)
        wd = jnp.where(row < valid, wd_ref[...], 0.0).astype(BF16)

        def swiglu_rows(rows):
            x = xb[rows]
            hg = jnp.dot(x, wg, preferred_element_type=F32)
            hu = jnp.dot(x, wu, preferred_element_type=F32)
            h = jnp.where(col < valid, (hg * _sigmoid(hg)) * hu, 0.0)
            acc[rows] += jnp.dot(h.astype(BF16), wd, preferred_element_type=F32)

        n_grain = sub // EXPERT_GRAIN
        for lo, _ in subs:
            left = n_valid - lo
            for k in range(1, n_grain + 1):
                below = left <= k * EXPERT_GRAIN if k < n_grain else True
                pl.when(jnp.logical_and(left > (k - 1) * EXPERT_GRAIN, below))(
                    functools.partial(swiglu_rows, slice(lo, lo + k * EXPERT_GRAIN)))

    @pl.when(jnp.logical_and(active, f == pl.num_programs(1) - 1))
    def _():
        def start(r):
            @pl.when(r < n_valid)
            def _():
                scatter_row(r, dst_ref[base + r]).start()

        def wait(r):
            @pl.when(r < n_valid)
            def _():
                scatter_row(r, 0).wait()

        n8 = ((n_valid + SUBLANES - 1) // SUBLANES) * SUBLANES
        for_rows(n8, start)
        for_rows(n8, wait)


def _experts(x, slots, exp_wg, exp_wu, exp_wd, *, layer, sub, n_sub, tf):
    M, D = x.shape
    _, E, _, F = exp_wg.shape
    n_slots = 2 * M
    tm = sub * n_sub
    max_tiles = n_slots // tm + E
    n_f = pl.cdiv(F, tf)

    es = slots[:, 2:4].astype(jnp.int32).reshape(n_slots)
    oh = (es[:, None] == jnp.arange(E, dtype=jnp.int32)[None, :]).astype(jnp.int32)
    csum = jnp.cumsum(oh, axis=0)
    rank = jnp.sum((csum - oh) * oh, axis=1)
    counts = csum[-1]
    tiles = (counts + tm - 1) // tm
    tile_end = jnp.cumsum(tiles)
    tile_start = tile_end - tiles
    pos = tile_start[es] * tm + rank
    sid = jnp.arange(n_slots, dtype=jnp.int32)
    tok = sid // 2
    src = jnp.zeros((max_tiles * tm,), jnp.int32).at[pos].set(tok)
    dst = jnp.zeros((max_tiles * tm,), jnp.int32).at[pos].set((sid % 2) * M + tok)
    tid = jnp.arange(max_tiles, dtype=jnp.int32)
    te = jnp.searchsorted(tile_end, jnp.minimum(tid, tile_end[-1] - 1), side="right")
    te = te.astype(jnp.int32)
    tv = jnp.clip(counts[te] - (tid - tile_start[te]) * tm, 0, tm)
    tv = jnp.where(tid < tile_end[-1], tv, 0).astype(jnp.int32)

    f_idx = lambda i, f, tv: jnp.where(tv[i] > 0, f, n_f - 1)
    up_map = lambda i, f, te, tv, s, d: (layer, te[i], 0, f_idx(i, f, tv))
    down_map = lambda i, f, te, tv, s, d: (layer, te[i], f_idx(i, f, tv), 0)
    grid_spec = pltpu.PrefetchScalarGridSpec(
        num_scalar_prefetch=4,
        grid=(max_tiles, n_f),
        in_specs=[pl.BlockSpec(memory_space=pl.ANY),
                  pl.BlockSpec((None, None, D, tf), up_map),
                  pl.BlockSpec((None, None, D, tf), up_map),
                  pl.BlockSpec((None, None, tf, D), down_map)],
        out_specs=pl.BlockSpec(memory_space=pl.ANY),
        scratch_shapes=[pltpu.VMEM((tm, D), BF16), pltpu.VMEM((tm, D), F32),
                        pltpu.SemaphoreType.DMA(()), pltpu.SemaphoreType.DMA(())],
    )
    return pl.pallas_call(
        functools.partial(_experts_kernel, n_ff=F, sub=sub),
        grid_spec=grid_spec,
        out_shape=jax.ShapeDtypeStruct((n_slots, D), F32),
        compiler_params=pltpu.CompilerParams(dimension_semantics=("arbitrary", "arbitrary"),
                                             vmem_limit_bytes=EXPERT_VMEM_LIMIT),
        name="moe_experts",
    )(te, tv, src, dst, x, exp_wg, exp_wu, exp_wd)


def _combine_kernel(x_ref, s_ref, o0_ref, o1_ref, g_ref, b_ref, o_ref, *, alpha):
    s = s_ref[...]
    f = s[:, 0:1] * o0_ref[...] + s[:, 1:2] * o1_ref[...]
    o_ref[...] = _layer_norm(alpha * x_ref[...] + f, g_ref[...], b_ref[...])


def _combine(x, slots, out2, g, b, *, alpha, tm):
    M, D = x.shape
    NP = slots.shape[1]
    nb = M // tm
    full = lambda a: pl.BlockSpec(a.shape, lambda i: (0,) * a.ndim)
    return pl.pallas_call(
        functools.partial(_combine_kernel, alpha=alpha),
        grid=(nb,),
        in_specs=[pl.BlockSpec((tm, D), lambda i: (i, 0)),
                  pl.BlockSpec((tm, NP), lambda i: (i, 0)),
                  pl.BlockSpec((tm, D), lambda i: (i, 0)),
                  pl.BlockSpec((tm, D), lambda i: (i + nb, 0)),
                  full(g), full(b)],
        out_specs=pl.BlockSpec((tm, D), lambda i: (i, 0)),
        out_shape=jax.ShapeDtypeStruct((M, D), F32),
        compiler_params=_cparams("parallel"),
        name="moe_combine_ln",
    )(x, slots, out2, out2, g, b)


def _pad_to(a, axis, mult):
    n = a.shape[axis]
    pad = (-n) % mult
    if pad == 0:
        return a
    widths = [(0, 0)] * a.ndim
    widths[axis] = (0, pad)
    return jnp.pad(a, widths)


def _block_diag(w, group):
    n, d, e = w.shape
    wg = w.reshape(n // group, group, d, e)
    eye = jnp.eye(group, dtype=w.dtype)
    return jnp.einsum('gqde,qr->gqdre', wg, eye).reshape(n // group, group * d, group * e)


def kernel(x_prompt, x_sample, state_lru_conv, state_lru_h, state_rwkv_shift, state_rwkv_wkv,
           w_in, conv_w, conv_b, lru_wa, lru_ba, lru_wx, lru_bx, lru_lambda,
           mu_shift, w0, w2, a0, a2, g2, k_k, k_a, r_k, gn_g, gn_b, w_o,
           ln1_g, ln1_b, ln2_g, ln2_b, ffn_wg, ffn_wu, ffn_wd,
           w_router, exp_wg, exp_wu, exp_wd):
    Bp, Tp, D = x_prompt.shape
    Bs, Ts, _ = x_sample.shape
    depth = w_in.shape[0]
    W = conv_w.shape[2]
    n_rkv = 3 * W
    n_main = 2 * W + n_rkv
    n_lr = w_in.shape[2] - n_main
    r_decay, r_aaa = w2.shape[1], a2.shape[1]
    n_lr_p = n_lr + (-n_lr) % LANES
    alpha = (2.0 * depth) ** 0.25
    lead = SAMPLE_TP - Ts
    Mp, Ms = Bp * Tp, Bs * Ts
    group = 2 * LANES // HEAD_DIM

    row = lambda a: a.reshape(1, -1)
    head_ones = jnp.kron(jnp.eye(W // HEAD_DIM, dtype=F32),
                         jnp.ones((HEAD_DIM, HEAD_DIM), F32)).astype(BF16)
    x = jnp.concatenate([x_prompt.reshape(Mp, D), x_sample.reshape(Ms, D)], axis=0)
    outs = {k: [] for k in ("pc", "ph", "ps", "pw", "sc", "sh", "ss", "sw")}

    zs = lambda *shape: jnp.zeros((depth, Bs) + shape, F32)
    shift_rows = lambda a: jnp.concatenate([zs(lead - 1, a.shape[-1]), a[:, :, None]], axis=2)
    st_main = jnp.concatenate(
        [jnp.concatenate([zs(lead - (CONV_W - 1), W), state_lru_conv], axis=2),
         zs(lead, W), shift_rows(state_rwkv_shift[:, :, :n_rkv])], axis=3)
    st_lr = shift_rows(_pad_to(state_rwkv_shift[:, :, n_rkv:], 2, LANES))
    h0_ext = jnp.concatenate([zs(lead - 1, W), state_lru_h[:, :, None], zs(Ts, W)], axis=2)
    h0_ext = h0_ext.reshape(depth, Bs * SAMPLE_TP, W)

    for l in range(depth):
        w_main = w_in[l, :, :n_main].astype(BF16)
        w_lr = _pad_to(w_in[l, :, n_main:], 1, LANES).astype(BF16)
        tm_proj = _row_tile(Mp + Ms, PROJ_ROWS)
        proj = _matmul(x, w_main, tm_proj, W)
        proj_lr = _matmul(x, w_lr, tm_proj, n_lr_p)

        proj_s = jnp.concatenate([st_main[l], proj[Mp:].reshape(Bs, Ts, n_main)], axis=1)
        proj_s = proj_s.reshape(Bs * SAMPLE_TP, n_main)
        lr_s = jnp.concatenate([st_lr[l], proj_lr[Mp:].reshape(Bs, Ts, n_lr_p)], axis=1)
        lr_s = lr_s.reshape(Bs * SAMPLE_TP, n_lr_p)

        wbd = jnp.concatenate([_block_diag(lru_wa[l], group), _block_diag(lru_wx[l], group)],
                              axis=2).astype(BF16)
        bias = jnp.stack([lru_ba[l], lru_bx[l]])
        lru_args = (conv_w[l], row(conv_b[l]), wbd, bias, row(lru_lambda[l]))
        lru_p, xtail, htail = _lru(proj, None, *lru_args, rows=Mp,
                                   blocks_per_seq=Tp // TIME_ROWS, tp=Tp, lead=0)
        lru_s, h_s = _lru(proj_s, h0_ext[l], *lru_args, rows=Bs * SAMPLE_TP, blocks_per_seq=1,
                          tp=SAMPLE_TP, lead=lead)

        mu = mu_shift[l, :n_rkv].reshape(3, W)
        mulr = row(_pad_to(mu_shift[l, n_rkv:], 0, LANES))
        zeros_lr = lambda n: jnp.zeros((n, W), F32)
        w2p = jnp.concatenate([w2[l], zeros_lr(n_lr_p - r_decay)], axis=0).astype(BF16)
        a2p = jnp.concatenate([zeros_lr(r_decay), a2[l], zeros_lr(n_lr_p - r_decay - r_aaa)],
                              axis=0).astype(BF16)
        g2p = jnp.concatenate([zeros_lr(r_decay + r_aaa), g2[l], zeros_lr(n_lr_p - n_lr)],
                              axis=0).astype(BF16)
        prep_params = (mu, mulr, row(w0[l]), row(a0[l]), row(k_k[l]), row(k_a[l]),
                       w2p, a2p, g2p, head_ones)
        ops_p, tails = _prep(proj, proj_lr, prep_params, rows=Mp, col0=2,
                             blocks_per_seq=Tp // TIME_ROWS, tp=Tp, lead=0, chunk=WKV_CHUNK,
                             op_dtype=BF16)
        ops_s, _ = _prep(proj_s, lr_s, prep_params, rows=Bs * SAMPLE_TP, col0=2,
                         blocks_per_seq=1, tp=SAMPLE_TP, lead=lead, chunk=SAMPLE_TP,
                         op_dtype=F32)
        head_args = (row(r_k[l]), row(gn_g[l]), row(gn_b[l]))
        rwkv_p, S_p = _wkv_prompt(ops_p, *head_args, n_seq=Bp, tp=Tp, rows_blk=WKV_ROWS,
                                  chunk=WKV_CHUNK, width=min(W, WKV_WIDTH))
        rwkv_s, S_s = _wkv_sample(ops_s, state_rwkv_wkv[l], *head_args, n_seq=Bs, tp=SAMPLE_TP,
                                  rows_blk=WKV_SAMPLE_ROWS, width=min(W, WKV_SAMPLE_WIDTH),
                                  group=WKV_SAMPLE_GROUP)

        real = lambda a: a.reshape(Bs, SAMPLE_TP, -1)[:, lead:].reshape(Ms, -1)
        x = _oproj(x, lru_p, rwkv_p, real(lru_s), real(rwkv_s), w_o[l].astype(BF16),
                   row(ln1_g[l]), row(ln1_b[l]), alpha=alpha, tm=256)

        outs["pc"].append(xtail[:, SUBLANES - (CONV_W - 1):])
        outs["ph"].append(htail[:, 0])
        outs["ps"].append(jnp.concatenate([t[:, -1] for t in tails[:3]]
                                          + [tails[3][:, -1, :n_lr]], axis=-1))
        outs["pw"].append(S_p)
        sm = proj_s.reshape(Bs, SAMPLE_TP, n_main)
        outs["sc"].append(sm[:, SAMPLE_TP - (CONV_W - 1):, :W])
        outs["sh"].append(h_s.reshape(Bs, SAMPLE_TP, W)[:, -1])
        outs["ss"].append(jnp.concatenate([sm[:, -1, 2 * W:],
                                           lr_s.reshape(Bs, SAMPLE_TP, n_lr_p)[:, -1, :n_lr]],
                                          axis=-1))
        outs["sw"].append(S_s)

        j = l // 2
        if l % 2 == 0:
            wg = _pad_to(ffn_wg[j], 1, FF_TILE).astype(BF16)
            wu = _pad_to(ffn_wu[j], 1, FF_TILE).astype(BF16)
            wd = _pad_to(ffn_wd[j], 0, FF_TILE).astype(BF16)
            x = _ffn(x, wg, wu, wd, row(ln2_g[l]), row(ln2_b[l]),
                     alpha=alpha, tm=TOKEN_TILE, tf=FF_TILE)
        else:
            slots = _router(x, _pad_to(w_router[j], 1, LANES), n_experts=exp_wg.shape[1],
                            tm=TOKEN_TILE)
            out2 = _experts(x, slots, exp_wg, exp_wu, exp_wd, layer=j, sub=EXPERT_SUB,
                            n_sub=EXPERT_SUBS, tf=EXPERT_FF_TILE)
            x = _combine(x, slots, out2, row(ln2_g[l]), row(ln2_b[l]), alpha=alpha, tm=TOKEN_TILE)

    y_prompt = x[:Mp].reshape(Bp, Tp, D)
    y_sample = x[Mp:].reshape(Bs, Ts, D)
    st = lambda k: jnp.stack(outs[k])
    return (y_prompt, y_sample, st("pc"), st("ph"), st("ps"), st("pw"),
            st("sc"), st("sh"), st("ss"), st("sw"))
```

```python
import functools

import jax
import jax.numpy as jnp
from jax import lax
from jax.experimental import pallas as pl
from jax.experimental.pallas import tpu as pltpu

F32 = jnp.float32
BF16 = jnp.bfloat16
HI = lax.Precision.HIGHEST

HEAD_DIM = 64
CONV_W = 4
LRU_C = 8.0
N_EXPERTS = 8
LN_EPS = 1e-5
GN_EPS = 64e-5

LANES = 128
SUBLANES = 8
VMEM_LIMIT = 48 * 1024 * 1024
SAMPLE_TP = 8
TIME_ROWS = 256
TOKEN_TILE = 512
PROJ_ROWS = 1088
FF_TILE = 512
EXPERT_SUB = 768
EXPERT_SUBS = 3
EXPERT_GRAIN = 256
EXPERT_FF_TILE = 256
EXPERT_VMEM_LIMIT = 56 * 1024 * 1024
WKV_CHUNK = 64
WKV_ROWS = 256
WKV_WIDTH = 1024
PRE_CHUNKS = 2
WKV_SAMPLE_ROWS = 128
WKV_SAMPLE_WIDTH = 256
WKV_SAMPLE_GROUP = 8


def _cparams(*sem):
    return pltpu.CompilerParams(dimension_semantics=sem, vmem_limit_bytes=VMEM_LIMIT)


def _sigmoid(x):
    return 1.0 / (1.0 + jnp.exp(-x))


def _softplus(x):
    return jnp.maximum(x, 0.0) + jnp.log1p(jnp.exp(-jnp.abs(x)))


def _gelu_tanh(x):
    c = 0.7978845608028654
    return 0.5 * x * (1.0 + jnp.tanh(c * (x + 0.044715 * (x * x * x))))


def _layer_norm(x, g, b):
    mu = jnp.mean(x, axis=-1, keepdims=True)
    xc = x - mu
    var = jnp.mean(xc * xc, axis=-1, keepdims=True)
    return xc * lax.rsqrt(var + LN_EPS) * g + b


def _shift_rows(x, carry, k):
    xr = pltpu.roll(x, k, 0)
    cr = pltpu.roll(carry, k, 0)
    row = lax.broadcasted_iota(jnp.int32, carry.shape, 0)
    first = jnp.where(row < k, cr, xr[:SUBLANES])
    return jnp.concatenate([first, xr[SUBLANES:]], axis=0)


def _layer_block(a, layer):
    return pl.BlockSpec((None,) + a.shape[1:], lambda *_: (layer,) + (0,) * (a.ndim - 1))


def _row_tile(m, target):
    return max(t for t in range(SUBLANES, target + 1, SUBLANES) if m % t == 0)


def _mm_kernel(x_ref, w_ref, o_ref, xb):
    @pl.when(pl.program_id(1) == 0)
    def _():
        xb[...] = x_ref[...].astype(BF16)

    o_ref[...] = jnp.dot(xb[...], w_ref[...], preferred_element_type=F32)


def _matmul(x, w, layer, tm, tn):
    M, K = x.shape
    N = w.shape[2]
    return pl.pallas_call(
        _mm_kernel,
        grid=(M // tm, N // tn),
        in_specs=[pl.BlockSpec((tm, K), lambda i, j: (i, 0)),
                  pl.BlockSpec((None, K, tn), lambda i, j: (layer, 0, j))],
        out_specs=pl.BlockSpec((tm, tn), lambda i, j: (i, j)),
        out_shape=jax.ShapeDtypeStruct((M, N), F32),
        scratch_shapes=[pltpu.VMEM((tm, K), BF16)],
        compiler_params=_cparams("parallel", "arbitrary"),
        name="in_proj",
    )(x, w)


def _lru_kernel(*refs, blocks_per_seq, tp, lead):
    if lead:
        (xb_ref, gate_ref, h0_ref, cw_ref, cb_ref, wbd_ref, bias_ref, lam_ref,
         out_ref, h_ref, xcarry, hcarry, a_s, u_s) = refs
    else:
        (xb_ref, gate_ref, cw_ref, cb_ref, wbd_ref, bias_ref, lam_ref,
         out_ref, xtail_ref, htail_ref, xcarry, hcarry, a_s, u_s, h_ref) = refs
    R, W = xb_ref.shape
    i = pl.program_id(0)

    @pl.when(i % blocks_per_seq == 0)
    def _():
        xcarry[...] = jnp.zeros_like(xcarry)
        hcarry[...] = jnp.zeros_like(hcarry)

    xb = xb_ref[...]
    carry = xcarry[...]
    cw = cw_ref[...]
    xc = cb_ref[...] + cw[3:4] * xb
    for k in range(1, CONV_W):
        xc = xc + cw[3 - k:4 - k] * _shift_rows(xb, carry, k)
    xcarry[...] = xb[R - SUBLANES:]

    xcb = xc.astype(BF16)
    nb = wbd_ref.shape[0]
    bw = W // nb
    pre_a, pre_x = [], []
    for g in range(nb):
        pre = jnp.dot(xcb[:, g * bw:(g + 1) * bw], wbd_ref[g], preferred_element_type=F32)
        pre_a.append(pre[:, :bw])
        pre_x.append(pre[:, bw:])
    bias = bias_ref[...]
    r = _sigmoid(jnp.concatenate(pre_a, axis=1) + bias[0:1])
    ig = _sigmoid(jnp.concatenate(pre_x, axis=1) + bias[1:2])
    log_a = -LRU_C * r * _softplus(-lam_ref[...])
    a = jnp.exp(log_a)
    u = jnp.sqrt(-jnp.tanh(log_a) * (a * a + 1.0)) * (ig * xc)
    if lead:
        row = lax.broadcasted_iota(jnp.int32, (R, W), 0)
        is_state = (row % tp) == (lead - 1)
        a = jnp.where(is_state, 0.0, a)
        u = jnp.where(is_state, h0_ref[...], u)
    a_s[...] = a
    u_s[...] = u

    def tile_body(t, h):
        off = pl.multiple_of(t * SUBLANES, SUBLANES)
        a8 = a_s[pl.ds(off, SUBLANES), :]
        u8 = u_s[pl.ds(off, SUBLANES), :]
        rows = []
        for j in range(SUBLANES):
            h = a8[j:j + 1] * h + u8[j:j + 1]
            rows.append(h)
        h_ref[pl.ds(off, SUBLANES), :] = jnp.concatenate(rows, axis=0)
        return h

    h_last = lax.fori_loop(0, R // SUBLANES, tile_body, hcarry[0:1])
    hcarry[...] = jnp.broadcast_to(h_last, hcarry.shape)
    out_ref[...] = h_ref[...] * _gelu_tanh(gate_ref[...])
    if not lead:
        xtail_ref[...] = xb[R - SUBLANES:]
        htail_ref[...] = hcarry[...]


def _lru(proj, h0_ext, cw, cb, wbd, bias, lam, *, layer, rows, blocks_per_seq, tp, lead):
    W = cw.shape[2]
    R = TIME_ROWS
    row_blk = lambda c: pl.BlockSpec((R, W), lambda i, c=c: (i, c))
    full = lambda a: _layer_block(a, layer)
    ins = [proj, proj]
    specs = [row_blk(0), row_blk(1)]
    scratch = [pltpu.VMEM((SUBLANES, W), F32), pltpu.VMEM((SUBLANES, W), F32),
               pltpu.VMEM((R, W), F32), pltpu.VMEM((R, W), F32)]
    if lead:
        ins.append(h0_ext)
        specs.append(row_blk(0))
        out_specs = [row_blk(0), row_blk(0)]
        out_shape = [jax.ShapeDtypeStruct((rows, W), F32)] * 2
    else:
        n_seq = rows // (R * blocks_per_seq)
        tail = pl.BlockSpec((None, SUBLANES, W), lambda i: (i // blocks_per_seq, 0, 0))
        out_specs = [row_blk(0), tail, tail]
        out_shape = ([jax.ShapeDtypeStruct((rows, W), F32)]
                     + [jax.ShapeDtypeStruct((n_seq, SUBLANES, W), F32)] * 2)
        scratch.append(pltpu.VMEM((R, W), F32))
    params = [cw, cb, wbd, bias, lam]
    ins += params
    specs += [full(p) for p in params]
    return pl.pallas_call(
        functools.partial(_lru_kernel, blocks_per_seq=blocks_per_seq, tp=tp, lead=lead),
        grid=(rows // R,),
        in_specs=specs,
        out_specs=out_specs,
        out_shape=out_shape,
        scratch_shapes=scratch,
        compiler_params=_cparams("arbitrary"),
        name="rg_lru",
    )(*ins)


def _prep_kernel(r_ref, k_ref, v_ref, lr_ref, mu_ref, mulr_ref, w0_ref, a0_ref, kkw_ref, kaw_ref,
                 w2_ref, a2_ref, g2_ref, ones_ref,
                 ro_ref, km_ref, vo_ref, lw_ref, cum_ref, kk_ref, bk_ref, g_ref,
                 rt_ref, kt_ref, vt_ref, lt_ref,
                 cr, ck, cv, clr, *, blocks_per_seq, tp, lead, chunk):
    i = pl.program_id(0)

    @pl.when(i % blocks_per_seq == 0)
    def _():
        for c in (cr, ck, cv, clr):
            c[...] = jnp.zeros_like(c)

    def token_shift(x_ref, c_ref, mu):
        x = x_ref[...]
        prev = _shift_rows(x, c_ref[...], 1)
        c_ref[...] = x[x.shape[0] - SUBLANES:]
        return x + (prev - x) * mu

    mu = mu_ref[...]
    r = token_shift(r_ref, cr, mu[0:1])
    k = token_shift(k_ref, ck, mu[1:2])
    v = token_shift(v_ref, cv, mu[2:3])
    xl = token_shift(lr_ref, clr, mulr_ref[...])
    dw = jnp.dot(jnp.tanh(xl).astype(BF16), w2_ref[...], preferred_element_type=F32)
    w_log = -_softplus(-(w0_ref[...] + dw)) - 0.5
    lw = -jnp.exp(w_log)
    da = jnp.dot(xl.astype(BF16), a2_ref[...], preferred_element_type=F32)
    ah = _sigmoid(a0_ref[...] + da)
    g_ref[...] = jnp.dot(_sigmoid(xl).astype(BF16), g2_ref[...],
                         preferred_element_type=F32).astype(g_ref.dtype)

    R = r.shape[0]
    if lead:
        t = lax.broadcasted_iota(jnp.int32, r.shape, 0) % tp
        real = t >= lead
        k = jnp.where(real, k, 0.0)
        v = jnp.where(real, v, 0.0)
        lw = jnp.where(real, lw, 0.0)

    kk = k * kkw_ref[...]
    sq = kk * kk
    sq_hi = sq.astype(BF16)
    sq_lo = (sq - sq_hi.astype(F32)).astype(BF16)
    ss = (jnp.dot(sq_hi, ones_ref[...], preferred_element_type=F32)
          + jnp.dot(sq_lo, ones_ref[...], preferred_element_type=F32))
    kk = kk / jnp.maximum(jnp.sqrt(ss), 1e-12)

    shift = chunk.bit_length() - 1
    rr = lax.broadcasted_iota(jnp.int32, (R, R), 0)
    cc = lax.broadcasted_iota(jnp.int32, (R, R), 1)
    same = jnp.right_shift(rr, shift) == jnp.right_shift(cc, shift)
    tri = jnp.where(jnp.logical_and(same, cc <= rr), 1.0, 0.0)
    cum_ref[...] = jnp.dot(tri, lw, precision=HI, preferred_element_type=F32)

    ro_ref[...] = r.astype(ro_ref.dtype)
    km_ref[...] = (k * (1.0 + (ah - 1.0) * kaw_ref[...])).astype(km_ref.dtype)
    vo_ref[...] = v.astype(vo_ref.dtype)
    lw_ref[...] = lw
    kk_ref[...] = kk.astype(kk_ref.dtype)
    bk_ref[...] = (kk * ah).astype(bk_ref.dtype)
    for t_ref, c_ref in ((rt_ref, cr), (kt_ref, ck), (vt_ref, cv), (lt_ref, clr)):
        t_ref[...] = c_ref[...]


def _prep(proj, proj_lr, params, *, layer, rows, col0, blocks_per_seq, tp, lead, chunk,
          op_dtype):
    W = params[2].shape[-1]
    LR = proj_lr.shape[1]
    R = TIME_ROWS
    n_grp = rows // (R * blocks_per_seq)
    row_blk = lambda c: pl.BlockSpec((R, W), lambda i, c=c: (i, c))
    full = lambda a: _layer_block(a, 0 if a is params[-1] else layer)
    tail = lambda n: pl.BlockSpec((None, SUBLANES, n), lambda i: (i // blocks_per_seq, 0, 0))
    outs = pl.pallas_call(
        functools.partial(_prep_kernel, blocks_per_seq=blocks_per_seq, tp=tp, lead=lead,
                          chunk=chunk),
        grid=(rows // R,),
        in_specs=[row_blk(col0), row_blk(col0 + 1), row_blk(col0 + 2),
                  pl.BlockSpec((R, LR), lambda i: (i, 0))] + [full(p) for p in params],
        out_specs=[row_blk(0)] * 8 + [tail(W)] * 3 + [tail(LR)],
        out_shape=([jax.ShapeDtypeStruct((rows, W), dt)
                    for dt in (op_dtype,) * 3 + (F32, F32) + (op_dtype,) * 3]
                   + [jax.ShapeDtypeStruct((n_grp, SUBLANES, W), F32)] * 3
                   + [jax.ShapeDtypeStruct((n_grp, SUBLANES, LR), F32)]),
        scratch_shapes=[pltpu.VMEM((SUBLANES, W), F32)] * 3 + [pltpu.VMEM((SUBLANES, LR), F32)],
        compiler_params=_cparams("arbitrary"),
        name="rwkv_prep",
    )(proj, proj, proj, proj_lr, *params)
    return outs[:8], outs[8:]


def _mm(a, b):
    return jnp.dot(a.astype(BF16), b.astype(BF16), preferred_element_type=F32)


def _mm_nt(a, b):
    return lax.dot_general(a.astype(BF16), b.astype(BF16), (((1,), (1,)), ((), ())),
                           preferred_element_type=F32)


def _mm_tn(a, b):
    return lax.dot_general(a.astype(BF16), b.astype(BF16), (((0,), (0,)), ((), ())),
                           preferred_element_type=F32)


def _pairs_pre(tiles):
    C = tiles[0][0].shape[0]
    N = HEAD_DIM
    n = range(len(tiles))
    h0 = lax.broadcasted_iota(jnp.int32, (C, LANES), 1) < N
    stack = lambda x: jnp.concatenate([jnp.where(h0, x, 0.0), jnp.where(h0, 0.0, x)], axis=0)
    ri = lax.broadcasted_iota(jnp.int32, (2 * C, 2 * C), 0)
    ci = lax.broadcasted_iota(jnp.int32, (2 * C, 2 * C), 1)
    strict = (ci & (C - 1)) < (ri & (C - 1))
    incl = (ci & (C - 1)) <= (ri & (C - 1))
    eye = jnp.where(ri == ci, 1.0, 0.0)

    lhs, rhs, vs, zs, gl = [], [], [], [], []
    for r, kmod, v, lw, cum, kk, bk in tiles:
        g_incl = jnp.exp(cum)
        g_excl = jnp.exp(cum - lw)
        g_inv = jnp.exp(-cum)
        g_rest = jnp.exp(cum[C - 1:C] - cum)
        lhs.append(jnp.concatenate([stack(-(kk * g_excl)), stack(r * g_incl)],
                                   axis=0).astype(BF16))
        rhs.append(jnp.concatenate([stack(bk * g_inv), stack(kmod * g_inv)],
                                   axis=0).astype(BF16))
        vs.append(stack(v).astype(BF16))
        zs.append(jnp.concatenate([stack(bk * g_rest), stack(kmod * g_rest)],
                                  axis=0).astype(BF16))
        gl.append(g_incl[C - 1:C])

    aa = [_mm_nt(lhs[i], rhs[i]) for i in n]
    a_ab = [jnp.where(strict, aa[i][:2 * C, :2 * C], 0.0) for i in n]
    a_ak = [jnp.where(strict, aa[i][:2 * C, 2 * C:], 0.0).astype(BF16) for i in n]
    arbk = [jnp.concatenate([jnp.where(incl, aa[i][2 * C:, :2 * C], 0.0),
                             jnp.where(incl, aa[i][2 * C:, 2 * C:], 0.0)],
                            axis=1).astype(BF16) for i in n]
    av = [_mm(a_ak[i], vs[i]) for i in n]

    t = [eye + a_ab[i] for i in n]
    n_iter = max(1, (C - 1).bit_length())
    if n_iter > 1:
        p = [_mm(a_ab[i], a_ab[i]) for i in n]
    for it in range(1, n_iter):
        if it + 1 < n_iter:
            pt = [_mm(p[i], jnp.concatenate([t[i], p[i]], axis=1)) for i in n]
            t = [t[i] + pt[i][:, :2 * C] for i in n]
            p = [pt[i][:, 2 * C:] for i in n]
        else:
            t = [t[i] + _mm(p[i], t[i]) for i in n]
    t = [t[i].astype(BF16) for i in n]
    tav = [_mm(t[i], av[i]) for i in n]
    return [(lhs[i], t[i], tav[i], arbk[i], vs[i], zs[i], gl[i]) for i in n]


def _pairs_step(states, pres):
    n = range(len(states))
    C2 = pres[0][1].shape[0]
    ls = [_mm_nt(pres[i][0], states[i]) for i in n]
    u = [_mm(pres[i][1], ls[i][:C2]) + pres[i][2] for i in n]
    xs = [jnp.concatenate([u[i].astype(BF16), pres[i][4]], axis=0) for i in n]
    s1 = [states[i] * pres[i][6] + _mm_tn(xs[i], pres[i][5]) for i in n]
    ys = [ls[i][C2:] + _mm(pres[i][3], xs[i]) for i in n]
    return [(ys[i][:C2 // 2] + ys[i][C2 // 2:], s1[i]) for i in n]


def _wkv_epilogue(y, r, kmod, v, g, rk, gg, gb):
    N = HEAD_DIM
    h0 = lax.broadcasted_iota(jnp.int32, y.shape, 1) < N

    def head_sum(x):
        s0 = jnp.sum(jnp.where(h0, x, 0.0), axis=-1, keepdims=True)
        s1 = jnp.sum(jnp.where(h0, 0.0, x), axis=-1, keepdims=True)
        return jnp.where(h0, s0, s1)

    yc = y - head_sum(y) * (1.0 / N)
    var = head_sum(yc * yc) * (1.0 / N)
    yn = yc * lax.rsqrt(var + GN_EPS) * gg + gb
    bonus = head_sum(r * kmod * rk) * v
    return (yn + bonus) * g


def _wkv_prompt_kernel(r_ref, km_ref, v_ref, lw_ref, cum_ref, kk_ref, bk_ref, g_ref,
                       rk_ref, gg_ref, gb_ref, o_ref, so_ref,
                       st, lhs_s, t_s, tav_s, arbk_s, vs_s, zs_s, gl_s, *, chunk):
    N = HEAD_DIM
    C = chunk
    rows_blk, width = r_ref.shape
    pairs = width // LANES
    n_chunks = rows_blk // C
    tb = pl.program_id(2)

    @pl.when(tb == 0)
    def _():
        st[...] = jnp.zeros_like(st)

    in_refs = (r_ref, km_ref, v_ref, lw_ref, cum_ref, kk_ref, bk_ref)
    pre_refs = (lhs_s, t_s, tav_s, arbk_s, vs_s, zs_s)
    lanes = [slice(p * LANES, (p + 1) * LANES) for p in range(pairs)]

    def pre_body(c2, _):
        where = [(c2 * PRE_CHUNKS + j, p) for j in range(PRE_CHUNKS) for p in range(pairs)]
        tiles = []
        for c, p in where:
            rows = pl.ds(pl.multiple_of(c * C, C), C)
            tiles.append(tuple(ref[rows, lanes[p]].astype(F32) for ref in in_refs))
        for (c, p), pre in zip(where, _pairs_pre(tiles)):
            for ref, val in zip(pre_refs, pre[:6]):
                ref[c, p] = val
            gl_s[c, p] = jnp.broadcast_to(pre[6], (SUBLANES, LANES))
        return 0

    lax.fori_loop(0, n_chunks // PRE_CHUNKS, pre_body, 0)

    def step_body(c, _):
        rows = pl.ds(pl.multiple_of(c * C, C), C)
        pres = [tuple(ref[c, p] for ref in pre_refs) + (gl_s[c, p][0:1],) for p in range(pairs)]
        outs = _pairs_step([st[p] for p in range(pairs)], pres)
        for p, (y, s1) in enumerate(outs):
            st[p] = s1
            r, kmod, v, g = (ref[rows, lanes[p]].astype(F32)
                             for ref in (r_ref, km_ref, v_ref, g_ref))
            o_ref[rows, lanes[p]] = _wkv_epilogue(
                y, r, kmod, v, g, rk_ref[:, lanes[p]], gg_ref[:, lanes[p]], gb_ref[:, lanes[p]])
        return 0

    lax.fori_loop(0, n_chunks, step_body, 0)

    @pl.when(tb == pl.num_programs(2) - 1)
    def _():
        for p in range(pairs):
            sb = st[p]
            so_ref[0, 2 * p] = sb[:N, :N]
            so_ref[0, 2 * p + 1] = sb[N:, N:]


def _wkv_sample_kernel(r_ref, km_ref, v_ref, lw_ref, cum_ref, kk_ref, bk_ref, g_ref, s0_ref,
                       rk_ref, gg_ref, gb_ref, o_ref, so_ref, *, tp, group):
    N = HEAD_DIM
    rows_blk, width = r_ref.shape
    pairs = width // LANES
    n_seq = rows_blk // tp
    zero = jnp.zeros((N, N), F32)

    in_refs = (r_ref, km_ref, v_ref, lw_ref, cum_ref, kk_ref, bk_ref)
    lanes = [slice(p * LANES, (p + 1) * LANES) for p in range(pairs)]

    def group_body(gi, _):
        where = [(gi * group + j, p) for j in range(group) for p in range(pairs)]
        tiles, states = [], []
        for s, p in where:
            rows = pl.ds(pl.multiple_of(s * tp, SUBLANES), tp)
            tiles.append(tuple(ref[rows, lanes[p]] for ref in in_refs))
            top = jnp.concatenate([s0_ref[s, 2 * p], zero], axis=1)
            bot = jnp.concatenate([zero, s0_ref[s, 2 * p + 1]], axis=1)
            states.append(jnp.concatenate([top, bot], axis=0))
        outs = _pairs_step(states, _pairs_pre(tiles))
        for (s, p), tile, (y, s1) in zip(where, tiles, outs):
            rows = pl.ds(pl.multiple_of(s * tp, SUBLANES), tp)
            so_ref[s, 2 * p] = s1[:N, :N]
            so_ref[s, 2 * p + 1] = s1[N:, N:]
            o_ref[rows, lanes[p]] = _wkv_epilogue(
                y, tile[0], tile[1], tile[2], g_ref[rows, lanes[p]],
                rk_ref[:, lanes[p]], gg_ref[:, lanes[p]], gb_ref[:, lanes[p]])
        return 0

    lax.fori_loop(0, n_seq // group, group_body, 0)


def _wkv_prompt(ops, rk_w, gg_w, gb_w, *, layer, n_seq, tp, rows_blk, chunk, width):
    rows, W = ops[0].shape
    N = HEAD_DIM
    H = W // N
    hq = width // N
    pairs = width // LANES
    tbs = tp // rows_blk
    n_chunks = rows_blk // chunk
    C2, C4 = 2 * chunk, 4 * chunk
    blk = pl.BlockSpec((rows_blk, width), lambda i, q, t: (i * tbs + t, q))
    pblk = pl.BlockSpec((None, 1, width), lambda i, q, t: (layer, 0, q))
    sblk = pl.BlockSpec((1, hq, N, N), lambda i, q, t: (i, q, 0, 0))
    per_chunk = lambda shape, dt: pltpu.VMEM((n_chunks, pairs) + shape, dt)
    return pl.pallas_call(
        functools.partial(_wkv_prompt_kernel, chunk=chunk),
        grid=(n_seq, H // hq, tbs),
        in_specs=[blk] * len(ops) + [pblk] * 3,
        out_specs=[blk, sblk],
        out_shape=[jax.ShapeDtypeStruct((rows, W), F32),
                   jax.ShapeDtypeStruct((n_seq, H, N, N), F32)],
        scratch_shapes=[pltpu.VMEM((pairs, LANES, LANES), F32),
                        per_chunk((C4, LANES), BF16), per_chunk((C2, C2), BF16),
                        per_chunk((C2, LANES), F32), per_chunk((C2, C4), BF16),
                        per_chunk((C2, LANES), BF16), per_chunk((C4, LANES), BF16),
                        per_chunk((SUBLANES, LANES), F32)],
        compiler_params=_cparams("parallel", "parallel", "arbitrary"),
        name="wkv7_prompt",
    )(*ops, rk_w, gg_w, gb_w)


def _wkv_sample(ops, s0, rk_w, gg_w, gb_w, *, layer, n_seq, tp, rows_blk, width, group):
    rows, W = ops[0].shape
    N = HEAD_DIM
    H = W // N
    hq = width // N
    seq_blk = rows_blk // tp
    blk = pl.BlockSpec((rows_blk, width), lambda i, q: (i, q))
    pblk = pl.BlockSpec((None, 1, width), lambda i, q: (layer, 0, q))
    sblk = pl.BlockSpec((seq_blk, hq, N, N), lambda i, q: (i, q, 0, 0))
    return pl.pallas_call(
        functools.partial(_wkv_sample_kernel, tp=tp, group=group),
        grid=(n_seq // seq_blk, H // hq),
        in_specs=[blk] * len(ops) + [sblk] + [pblk] * 3,
        out_specs=[blk, sblk],
        out_shape=[jax.ShapeDtypeStruct((rows, W), F32),
                   jax.ShapeDtypeStruct((n_seq, H, N, N), F32)],
        compiler_params=_cparams("parallel", "parallel"),
        name="wkv7_sample",
    )(*ops, s0, rk_w, gg_w, gb_w)


def _oproj_kernel(x_ref, ap_ref, bp_ref, as_ref, bs_ref, w_ref, g_ref, bb_ref, o_ref, *,
                  alpha, n_p):
    half = ap_ref.shape[1]
    first = pl.program_id(0) < n_p
    a = jnp.where(first, ap_ref[...], as_ref[...]).astype(BF16)
    b = jnp.where(first, bp_ref[...], bs_ref[...]).astype(BF16)
    m = jnp.dot(a, w_ref[:half], preferred_element_type=F32)
    m = m + jnp.dot(b, w_ref[half:], preferred_element_type=F32)
    o_ref[...] = _layer_norm(alpha * x_ref[...] + m, g_ref[...], bb_ref[...])


def _oproj(x, lru_p, rwkv_p, lru_s, rwkv_s, w_o, g, b, *, layer, alpha, tm):
    M, D = x.shape
    Wh = lru_p.shape[1]
    n_p = lru_p.shape[0] // tm
    full = lambda a: _layer_block(a, layer)
    first = pl.BlockSpec((tm, Wh), lambda i: (jnp.minimum(i, n_p - 1), 0))
    second = pl.BlockSpec((tm, Wh), lambda i: (jnp.maximum(i - n_p, 0), 0))
    return pl.pallas_call(
        functools.partial(_oproj_kernel, alpha=alpha, n_p=n_p),
        grid=(M // tm,),
        in_specs=[pl.BlockSpec((tm, D), lambda i: (i, 0)), first, first, second, second,
                  full(w_o), full(g), full(b)],
        out_specs=pl.BlockSpec((tm, D), lambda i: (i, 0)),
        out_shape=jax.ShapeDtypeStruct((M, D), F32),
        compiler_params=_cparams("parallel"),
        name="out_proj_ln",
    )(x, lru_p, rwkv_p, lru_s, rwkv_s, w_o, g, b)


def _ffn_kernel(x_ref, wg_ref, wu_ref, wd_ref, g_ref, b_ref, o_ref, xb, acc, *, alpha):
    f = pl.program_id(1)

    @pl.when(f == 0)
    def _():
        xb[...] = x_ref[...].astype(BF16)
        acc[...] = jnp.zeros_like(acc)

    x = xb[...]
    hg = jnp.dot(x, wg_ref[...], preferred_element_type=F32)
    hu = jnp.dot(x, wu_ref[...], preferred_element_type=F32)
    h = (hg * _sigmoid(hg)) * hu
    acc[...] += jnp.dot(h.astype(BF16), wd_ref[...], preferred_element_type=F32)

    @pl.when(f == pl.num_programs(1) - 1)
    def _():
        o_ref[...] = _layer_norm(alpha * x_ref[...] + acc[...], g_ref[...], b_ref[...])


def _ffn(x, wg, wu, wd, g, b, *, layer, ffn_layer, alpha, tm, tf):
    M, D = x.shape
    F = wg.shape[2]
    full = lambda a: _layer_block(a, layer)
    return pl.pallas_call(
        functools.partial(_ffn_kernel, alpha=alpha),
        grid=(M // tm, F // tf),
        in_specs=[pl.BlockSpec((tm, D), lambda i, f: (i, 0)),
                  pl.BlockSpec((None, D, tf), lambda i, f: (ffn_layer, 0, f)),
                  pl.BlockSpec((None, D, tf), lambda i, f: (ffn_layer, 0, f)),
                  pl.BlockSpec((None, tf, D), lambda i, f: (ffn_layer, f, 0)),
                  full(g), full(b)],
        out_specs=pl.BlockSpec((tm, D), lambda i, f: (i, 0)),
        out_shape=jax.ShapeDtypeStruct((M, D), F32),
        scratch_shapes=[pltpu.VMEM((tm, D), BF16), pltpu.VMEM((tm, D), F32)],
        compiler_params=_cparams("parallel", "arbitrary"),
        name="ffn_ln",
    )(x, wg, wu, wd, g, b)


def _router_kernel(x_ref, w_ref, o_ref, *, n_experts):
    logits = jnp.dot(x_ref[...], w_ref[...], precision=HI, preferred_element_type=F32)
    col = lax.broadcasted_iota(jnp.int32, logits.shape, 1)
    neg = jnp.float32(-jnp.inf)
    logits = jnp.where(col < n_experts, logits, neg)
    big = jnp.int32(logits.shape[1])
    m1 = jnp.max(logits, axis=-1, keepdims=True)
    i1 = jnp.min(jnp.where(logits == m1, col, big), axis=-1, keepdims=True)
    rest = jnp.where(col == i1, neg, logits)
    m2 = jnp.max(rest, axis=-1, keepdims=True)
    i2 = jnp.min(jnp.where(rest == m2, col, big), axis=-1, keepdims=True)
    e2 = jnp.exp(m2 - m1)
    p1 = 1.0 / (1.0 + e2)
    p2 = e2 / (1.0 + e2)
    o_ref[...] = (jnp.where(col == 0, p1, 0.0) + jnp.where(col == 1, p2, 0.0)
                  + jnp.where(col == 2, i1.astype(F32), 0.0)
                  + jnp.where(col == 3, i2.astype(F32), 0.0))


def _router(x, w_router_p, *, layer, n_experts, tm):
    M, D = x.shape
    NP = w_router_p.shape[2]
    return pl.pallas_call(
        functools.partial(_router_kernel, n_experts=n_experts),
        grid=(M // tm,),
        in_specs=[pl.BlockSpec((tm, D), lambda i: (i, 0)), _layer_block(w_router_p, layer)],
        out_specs=pl.BlockSpec((tm, NP), lambda i: (i, 0)),
        out_shape=jax.ShapeDtypeStruct((M, NP), F32),
        compiler_params=_cparams("parallel"),
        name="router_top2",
    )(x, w_router_p)


def _experts_kernel(te_ref, tv_ref, src_ref, dst_ref, x_hbm, wg_ref, wu_ref, wd_ref,
                    out_hbm, xb, acc, gsem, ssem, *, n_ff, sub):
    i = pl.program_id(0)
    f = pl.program_id(1)
    S = acc.shape[0]
    tf = wg_ref.shape[1]
    n_valid = tv_ref[i]
    active = n_valid > 0
    base = i * S
    subs = [(s * sub, slice(s * sub, (s + 1) * sub)) for s in range(S // sub)]

    def gather_row(r, tok):
        return pltpu.make_async_copy(x_hbm.at[pl.ds(tok, 1)], acc.at[pl.ds(r, 1)], gsem)

    def scatter_row(r, dst):
        return pltpu.make_async_copy(acc.at[pl.ds(r, 1)], out_hbm.at[pl.ds(dst, 1)], ssem)

    def for_rows(n, fn):
        def body(g, _):
            for j in range(SUBLANES):
                fn(g * SUBLANES + j)
            return 0
        lax.fori_loop(0, n // SUBLANES, body, 0)

    @pl.when(jnp.logical_and(active, f == 0))
    def _():
        n_rows = sum(jnp.where(lo < n_valid, sub, 0) for lo, _ in subs)
        for_rows(n_rows, lambda r: gather_row(r, src_ref[base + r]).start())
        for_rows(n_rows, lambda r: gather_row(r, 0).wait())
        for lo, rows in subs:
            @pl.when(lo < n_valid)
            def _():
                xb[rows] = acc[rows].astype(BF16)
                acc[rows] = jnp.zeros((sub, acc.shape[1]), F32)

    @pl.when(active)
    def _():
        valid = n_ff - f * tf
        col = lax.broadcasted_iota(jnp.int32, (1, tf), 1)
        row = lax.broadcasted_iota(jnp.int32, (tf, 1), 0)
        wg = wg_ref[...].astype(BF16)
        wu = wu_ref[...].astype(BF16)
        wd = jnp.where(row < valid, wd_ref[...], 0.0).astype(BF16)

        def swiglu_rows(rows):
            x = xb[rows]
            hg = jnp.dot(x, wg, preferred_element_type=F32)
            hu = jnp.dot(x, wu, preferred_element_type=F32)
            h = jnp.where(col < valid, (hg * _sigmoid(hg)) * hu, 0.0)
            acc[rows] += jnp.dot(h.astype(BF16), wd, preferred_element_type=F32)

        n_grain = sub // EXPERT_GRAIN
        for lo, _ in subs:
            left = n_valid - lo
            for k in range(1, n_grain + 1):
                below = left <= k * EXPERT_GRAIN if k < n_grain else True
                pl.when(jnp.logical_and(left > (k - 1) * EXPERT_GRAIN, below))(
                    functools.partial(swiglu_rows, slice(lo, lo + k * EXPERT_GRAIN)))

    @pl.when(jnp.logical_and(active, f == pl.num_programs(1) - 1))
    def _():
        def start(r):
            @pl.when(r < n_valid)
            def _():
                scatter_row(r, dst_ref[base + r]).start()

        def wait(r):
            @pl.when(r < n_valid)
            def _():
                scatter_row(r, 0).wait()

        n8 = ((n_valid + SUBLANES - 1) // SUBLANES) * SUBLANES
        for_rows(n8, start)
        for_rows(n8, wait)


def _experts(x, slots, exp_wg, exp_wu, exp_wd, *, layer, sub, n_sub, tf):
    M, D = x.shape
    _, E, _, F = exp_wg.shape
    n_slots = 2 * M
    tm = sub * n_sub
    max_tiles = n_slots // tm + E
    n_f = pl.cdiv(F, tf)

    es = slots[:, 2:4].astype(jnp.int32).reshape(n_slots)
    oh = (es[:, None] == jnp.arange(E, dtype=jnp.int32)[None, :]).astype(jnp.int32)
    csum = jnp.cumsum(oh, axis=0)
    rank = jnp.sum((csum - oh) * oh, axis=1)
    counts = csum[-1]
    tiles = (counts + tm - 1) // tm
    tile_end = jnp.cumsum(tiles)
    tile_start = tile_end - tiles
    pos = tile_start[es] * tm + rank
    sid = jnp.arange(n_slots, dtype=jnp.int32)
    slot_at = jnp.zeros((max_tiles * tm,), jnp.int32).at[pos].set(sid)
    src = slot_at // 2
    dst = (slot_at % 2) * M + src
    tid = jnp.arange(max_tiles, dtype=jnp.int32)
    te = jnp.searchsorted(tile_end, jnp.minimum(tid, tile_end[-1] - 1), side="right")
    te = te.astype(jnp.int32)
    tv = jnp.clip(counts[te] - (tid - tile_start[te]) * tm, 0, tm)
    tv = jnp.where(tid < tile_end[-1], tv, 0).astype(jnp.int32)

    f_idx = lambda i, f, tv: jnp.where(tv[i] > 0, f, n_f - 1)
    up_map = lambda i, f, te, tv, s, d: (layer, te[i], 0, f_idx(i, f, tv))
    down_map = lambda i, f, te, tv, s, d: (layer, te[i], f_idx(i, f, tv), 0)
    grid_spec = pltpu.PrefetchScalarGridSpec(
        num_scalar_prefetch=4,
        grid=(max_tiles, n_f),
        in_specs=[pl.BlockSpec(memory_space=pl.ANY),
                  pl.BlockSpec((None, None, D, tf), up_map),
                  pl.BlockSpec((None, None, D, tf), up_map),
                  pl.BlockSpec((None, None, tf, D), down_map)],
        out_specs=pl.BlockSpec(memory_space=pl.ANY),
        scratch_shapes=[pltpu.VMEM((tm, D), BF16), pltpu.VMEM((tm, D), F32),
                        pltpu.SemaphoreType.DMA(()), pltpu.SemaphoreType.DMA(())],
    )
    return pl.pallas_call(
        functools.partial(_experts_kernel, n_ff=F, sub=sub),
        grid_spec=grid_spec,
        out_shape=jax.ShapeDtypeStruct((n_slots, D), F32),
        compiler_params=pltpu.CompilerParams(dimension_semantics=("arbitrary", "arbitrary"),
                                             vmem_limit_bytes=EXPERT_VMEM_LIMIT),
        name="moe_experts",
    )(te, tv, src, dst, x, exp_wg, exp_wu, exp_wd)


def _combine_kernel(x_ref, s_ref, o0_ref, o1_ref, g_ref, b_ref, o_ref, *, alpha):
    s = s_ref[...]
    f = s[:, 0:1] * o0_ref[...] + s[:, 1:2] * o1_ref[...]
    o_ref[...] = _layer_norm(alpha * x_ref[...] + f, g_ref[...], b_ref[...])


def _combine(x, slots, out2, g, b, *, layer, alpha, tm):
    M, D = x.shape
    NP = slots.shape[1]
    nb = M // tm
    full = lambda a: _layer_block(a, layer)
    return pl.pallas_call(
        functools.partial(_combine_kernel, alpha=alpha),
        grid=(nb,),
        in_specs=[pl.BlockSpec((tm, D), lambda i: (i, 0)),
                  pl.BlockSpec((tm, NP), lambda i: (i, 0)),
                  pl.BlockSpec((tm, D), lambda i: (i, 0)),
                  pl.BlockSpec((tm, D), lambda i: (i + nb, 0)),
                  full(g), full(b)],
        out_specs=pl.BlockSpec((tm, D), lambda i: (i, 0)),
        out_shape=jax.ShapeDtypeStruct((M, D), F32),
        compiler_params=_cparams("parallel"),
        name="moe_combine_ln",
    )(x, slots, out2, out2, g, b)


def _pad_to(a, axis, mult):
    n = a.shape[axis]
    pad = (-n) % mult
    if pad == 0:
        return a
    widths = [(0, 0)] * a.ndim
    widths[axis] = (0, pad)
    return jnp.pad(a, widths)


def _block_diag(w, group):
    n, d, e = w.shape
    wg = w.reshape(n // group, group, d, e)
    eye = jnp.eye(group, dtype=w.dtype)
    return jnp.einsum('gqde,qr->gqdre', wg, eye).reshape(n // group, group * d, group * e)


def kernel(x_prompt, x_sample, state_lru_conv, state_lru_h, state_rwkv_shift, state_rwkv_wkv,
           w_in, conv_w, conv_b, lru_wa, lru_ba, lru_wx, lru_bx, lru_lambda,
           mu_shift, w0, w2, a0, a2, g2, k_k, k_a, r_k, gn_g, gn_b, w_o,
           ln1_g, ln1_b, ln2_g, ln2_b, ffn_wg, ffn_wu, ffn_wd,
           w_router, exp_wg, exp_wu, exp_wd):
    Bp, Tp, D = x_prompt.shape
    Bs, Ts, _ = x_sample.shape
    depth = w_in.shape[0]
    W = conv_w.shape[2]
    n_rkv = 3 * W
    n_main = 2 * W + n_rkv
    n_lr = w_in.shape[2] - n_main
    r_decay, r_aaa = w2.shape[1], a2.shape[1]
    n_lr_p = n_lr + (-n_lr) % LANES
    alpha = (2.0 * depth) ** 0.25
    lead = SAMPLE_TP - Ts
    Mp, Ms = Bp * Tp, Bs * Ts
    group = 2 * LANES // HEAD_DIM

    rows3 = lambda a: a.reshape(a.shape[0], 1, -1)
    head_ones = jnp.kron(jnp.eye(W // HEAD_DIM, dtype=F32),
                         jnp.ones((HEAD_DIM, HEAD_DIM), F32)).astype(BF16)[None]
    w_main = w_in[:, :, :n_main].astype(BF16)
    w_lr = _pad_to(w_in[:, :, n_main:], 2, LANES).astype(BF16)
    block_diag = jax.vmap(lambda w: _block_diag(w, group))
    lru_params = (conv_w, rows3(conv_b),
                  jnp.concatenate([block_diag(lru_wa), block_diag(lru_wx)], axis=3).astype(BF16),
                  jnp.stack([lru_ba, lru_bx], axis=1), rows3(lru_lambda))
    zeros_lr = lambda n: jnp.zeros((depth, n, W), F32)
    prep_params = (
        mu_shift[:, :n_rkv].reshape(depth, 3, W), rows3(_pad_to(mu_shift[:, n_rkv:], 1, LANES)),
        rows3(w0), rows3(a0), rows3(k_k), rows3(k_a),
        jnp.concatenate([w2, zeros_lr(n_lr_p - r_decay)], axis=1).astype(BF16),
        jnp.concatenate([zeros_lr(r_decay), a2, zeros_lr(n_lr_p - r_decay - r_aaa)],
                        axis=1).astype(BF16),
        jnp.concatenate([zeros_lr(r_decay + r_aaa), g2, zeros_lr(n_lr_p - n_lr)],
                        axis=1).astype(BF16),
        head_ones)
    head_params = (rows3(r_k), rows3(gn_g), rows3(gn_b))
    w_o_b = w_o.astype(BF16)
    ln1, ln2 = (rows3(ln1_g), rows3(ln1_b)), (rows3(ln2_g), rows3(ln2_b))
    ffn_w = (_pad_to(ffn_wg, 2, FF_TILE).astype(BF16), _pad_to(ffn_wu, 2, FF_TILE).astype(BF16),
             _pad_to(ffn_wd, 1, FF_TILE).astype(BF16))
    w_router_p = _pad_to(w_router, 2, LANES)
    tm_proj = _row_tile(Mp + Ms, PROJ_ROWS)

    x = jnp.concatenate([x_prompt.reshape(Mp, D), x_sample.reshape(Ms, D)], axis=0)
    outs = {k: [] for k in ("pc", "ph", "ps", "pw", "sc", "sh", "ss", "sw")}

    zs = lambda *shape: jnp.zeros((depth, Bs) + shape, F32)
    shift_rows = lambda a: jnp.concatenate([zs(lead - 1, a.shape[-1]), a[:, :, None]], axis=2)
    st_main = jnp.concatenate(
        [jnp.concatenate([zs(lead - (CONV_W - 1), W), state_lru_conv], axis=2),
         zs(lead, W), shift_rows(state_rwkv_shift[:, :, :n_rkv])], axis=3)
    st_lr = shift_rows(_pad_to(state_rwkv_shift[:, :, n_rkv:], 2, LANES))
    h0_ext = jnp.concatenate([zs(lead - 1, W), state_lru_h[:, :, None], zs(Ts, W)], axis=2)
    h0_ext = h0_ext.reshape(depth, Bs * SAMPLE_TP, W)

    for l in range(depth):
        proj = _matmul(x, w_main, l, tm_proj, W)
        proj_lr = _matmul(x, w_lr, l, tm_proj, n_lr_p)

        proj_s = jnp.concatenate([st_main[l], proj[Mp:].reshape(Bs, Ts, n_main)], axis=1)
        proj_s = proj_s.reshape(Bs * SAMPLE_TP, n_main)
        lr_s = jnp.concatenate([st_lr[l], proj_lr[Mp:].reshape(Bs, Ts, n_lr_p)], axis=1)
        lr_s = lr_s.reshape(Bs * SAMPLE_TP, n_lr_p)

        lru_p, xtail, htail = _lru(proj, None, *lru_params, layer=l, rows=Mp,
                                   blocks_per_seq=Tp // TIME_ROWS, tp=Tp, lead=0)
        lru_s, h_s = _lru(proj_s, h0_ext[l], *lru_params, layer=l, rows=Bs * SAMPLE_TP,
                          blocks_per_seq=1, tp=SAMPLE_TP, lead=lead)

        ops_p, tails = _prep(proj, proj_lr, prep_params, layer=l, rows=Mp, col0=2,
                             blocks_per_seq=Tp // TIME_ROWS, tp=Tp, lead=0, chunk=WKV_CHUNK,
                             op_dtype=BF16)
        ops_s, _ = _prep(proj_s, lr_s, prep_params, layer=l, rows=Bs * SAMPLE_TP, col0=2,
                         blocks_per_seq=1, tp=SAMPLE_TP, lead=lead, chunk=SAMPLE_TP,
                         op_dtype=F32)
        rwkv_p, S_p = _wkv_prompt(ops_p, *head_params, layer=l, n_seq=Bp, tp=Tp,
                                  rows_blk=WKV_ROWS, chunk=WKV_CHUNK, width=min(W, WKV_WIDTH))
        rwkv_s, S_s = _wkv_sample(ops_s, state_rwkv_wkv[l], *head_params, layer=l, n_seq=Bs,
                                  tp=SAMPLE_TP, rows_blk=WKV_SAMPLE_ROWS,
                                  width=min(W, WKV_SAMPLE_WIDTH), group=WKV_SAMPLE_GROUP)

        real = lambda a: a.reshape(Bs, SAMPLE_TP, -1)[:, lead:].reshape(Ms, -1)
        x = _oproj(x, lru_p, rwkv_p, real(lru_s), real(rwkv_s), w_o_b, *ln1,
                   layer=l, alpha=alpha, tm=256)

        outs["pc"].append(xtail[:, SUBLANES - (CONV_W - 1):])
        outs["ph"].append(htail[:, 0])
        outs["ps"].append(jnp.concatenate([t[:, -1] for t in tails[:3]]
                                          + [tails[3][:, -1, :n_lr]], axis=-1))
        outs["pw"].append(S_p)
        sm = proj_s.reshape(Bs, SAMPLE_TP, n_main)
        outs["sc"].append(sm[:, SAMPLE_TP - (CONV_W - 1):, :W])
        outs["sh"].append(h_s.reshape(Bs, SAMPLE_TP, W)[:, -1])
        outs["ss"].append(jnp.concatenate([sm[:, -1, 2 * W:],
                                           lr_s.reshape(Bs, SAMPLE_TP, n_lr_p)[:, -1, :n_lr]],
                                          axis=-1))
        outs["sw"].append(S_s)

        j = l // 2
        if l % 2 == 0:
            x = _ffn(x, *ffn_w, *ln2, layer=l, ffn_layer=j, alpha=alpha, tm=TOKEN_TILE,
                     tf=FF_TILE)
        else:
            slots = _router(x, w_router_p, layer=j, n_experts=exp_wg.shape[1], tm=TOKEN_TILE)
            out2 = _experts(x, slots, exp_wg, exp_wu, exp_wd, layer=j, sub=EXPERT_SUB,
                            n_sub=EXPERT_SUBS, tf=EXPERT_FF_TILE)
            x = _combine(x, slots, out2, *ln2, layer=l, alpha=alpha, tm=TOKEN_TILE)

    y_prompt = x[:Mp].reshape(Bp, Tp, D)
    y_sample = x[Mp:].reshape(Bs, Ts, D)
    st = lambda k: jnp.stack(outs[k])
    return (y_prompt, y_sample, st("pc"), st("ph"), st("ps"), st("pw"),
            st("sc"), st("sh"), st("ss"), st("sw"))
```
